```python
import math
import jax, jax.numpy as jnp
from jax import lax
import numpy as np

D_MODEL = 1024
BATCH = 2
SEQ = 8192
DEPTH = 2
DEC_BATCH = 32
DEC_SEQ = 8
PAST_LEN = 16384
PAGE_SIZE = 128

N_HEADS = 8
N_KV_HEADS = 4
GROUP = N_HEADS // N_KV_HEADS
HEAD_DIM = 64
V_DIM = 2 * HEAD_DIM
Q_WIDTH = N_HEADS * 2 * HEAD_DIM
KV_WIDTH = N_KV_HEADS * V_DIM
ATT_WIDTH = N_HEADS * V_DIM
ROT_DIM = HEAD_DIM // 4
ROPE_THETA = 500000.0
Q_BLOCK = 128
D_CONV = D_MODEL
CONV_WIDTH = 3
D_FF = 2816
N_EXPERTS = 8
TOP_K = 2
D_FF_EXPERT = 3584
N_DENSE = (DEPTH + 1) // 2
N_MOE = DEPTH // 2
EPS = 1e-6
IN_WIDTH = 3 * D_CONV + Q_WIDTH + 2 * KV_WIDTH + 2 * D_MODEL
SPLIT_POINTS = (D_CONV, 2 * D_CONV, 3 * D_CONV, 3 * D_CONV + Q_WIDTH,
                3 * D_CONV + Q_WIDTH + KV_WIDTH, 3 * D_CONV + Q_WIDTH + 2 * KV_WIDTH,
                3 * D_CONV + Q_WIDTH + 2 * KV_WIDTH + D_MODEL)

kernel_name = 'hybrid_conv_diffattn_moe_adaln_step'


def _rmsnorm(x, g):
    xf = x.astype(jnp.float32)
    y = xf * lax.rsqrt(jnp.mean(xf * xf, axis=-1, keepdims=True) + EPS)
    return (y * g.astype(jnp.float32)).astype(x.dtype)


def _rope(x, pos):
    half = ROT_DIM // 2
    inv = ROPE_THETA ** (-jnp.arange(half, dtype=jnp.float32) / half)
    ang = pos.astype(jnp.float32)[:, None] * inv[None, :]
    cos = jnp.cos(ang)[:, None, None, :]
    sin = jnp.sin(ang)[:, None, None, :]
    xr = x[..., :ROT_DIM].astype(jnp.float32)
    x1, x2 = xr[..., :half], xr[..., half:]
    rot = jnp.concatenate([x1 * cos - x2 * sin, x2 * cos + x1 * sin], axis=-1).astype(x.dtype)
    return jnp.concatenate([rot, x[..., ROT_DIM:]], axis=-1)


def _diff_attn_block(q, k, v, q_pos, k_pos, lam):
    s = jnp.einsum('bqhgcd,bkhcd->bhgcqk', q, k).astype(jnp.float32) * (HEAD_DIM ** -0.5)
    s = jnp.where(k_pos[None, :] <= q_pos[:, None], s, -jnp.inf)
    a = jax.nn.softmax(s, axis=-1)
    a = a[:, :, :, 0] - lam * a[:, :, :, 1]
    return jnp.einsum('bhgqk,bkhe->bqhge', a.astype(v.dtype), v)


def _diff_attn(q, k, v, q_pos, k_pos, lam):
    b, nq = q.shape[:2]
    if nq > Q_BLOCK and nq % Q_BLOCK == 0:
        nb = nq // Q_BLOCK
        qb = jnp.moveaxis(q.reshape((b, nb, Q_BLOCK) + q.shape[2:]), 1, 0)
        pb = q_pos.reshape(nb, Q_BLOCK)
        o = lax.map(lambda blk: _diff_attn_block(blk[0], k, v, blk[1], k_pos, lam), (qb, pb))
        return jnp.moveaxis(o, 0, 1).reshape((b, nq) + o.shape[3:])
    return _diff_attn_block(q, k, v, q_pos, k_pos, lam)


def _causal_conv(u_pad, w):
    n = u_pad.shape[1] - (CONV_WIDTH - 1)
    out = w[0] * u_pad[:, 0:n]
    for j in range(1, CONV_WIDTH):
        out = out + w[j] * u_pad[:, j:j + n]
    return out, u_pad[:, n:]


def _swiglu(h, wg, wu, wd):
    return (jax.nn.silu(h @ wg) * (h @ wu)) @ wd


def _moe(h, w_router, wg, wu, wd):
    logits = (h @ w_router).astype(jnp.float32)
    top_v, top_i = lax.top_k(logits, TOP_K)
    gates = jax.nn.softmax(top_v, axis=-1)
    comb = jnp.sum(jax.nn.one_hot(top_i, N_EXPERTS, dtype=jnp.float32) * gates[..., None], axis=-2)
    out = jnp.zeros_like(h)
    for e in range(N_EXPERTS):
        out = out + comb[..., e:e + 1].astype(h.dtype) * _swiglu(h, wg[e], wu[e], wd[e])
    return out


def _layer(x, c, l, p, pos, attend, conv):
    b, s = x.shape[:2]
    mod = jnp.einsum('bd,de->be', jax.nn.silu(c), p['w_ada'][l]) + p['b_ada'][l]
    sh1, sc1, g1, sh2, sc2, g2 = jnp.split(mod[:, None, :], 6, axis=-1)
    h = _rmsnorm(x, p['norm1_g'][l]) * (1 + sc1) + sh1
    z = h @ p['w_in'][l]
    zb, zc, zv, zq, zk, zvv, zga, zgb = jnp.split(z, SPLIT_POINTS, axis=-1)
    cu, conv_state = conv(zc * zv, p['conv_w'][l], l)
    y_conv = zb * cu
    q = _rope(zq.reshape(b, s, N_HEADS, 2, HEAD_DIM), pos).reshape(b, s, N_KV_HEADS, GROUP, 2, HEAD_DIM)
    k = _rope(zk.reshape(b, s, N_KV_HEADS, 2, HEAD_DIM), pos)
    v = zvv.reshape(b, s, N_KV_HEADS, V_DIM)
    lam_init = 0.8 - 0.6 * math.exp(-0.3 * l)
    lam = (jnp.exp(jnp.sum(p['lambda_q1'][l].astype(jnp.float32) * p['lambda_k1'][l].astype(jnp.float32)))
           - jnp.exp(jnp.sum(p['lambda_q2'][l].astype(jnp.float32) * p['lambda_k2'][l].astype(jnp.float32)))
           + lam_init)
    o = attend(q, k, v, l, lam)
    o = _rmsnorm(o, p['subln_g'][l]) * (1.0 - lam_init)
    y_attn = o.reshape(b, s, ATT_WIDTH)
    m = (jax.nn.sigmoid(zga) * (y_conv @ p['w_br_conv'][l])
         + jax.nn.sigmoid(zgb) * (y_attn @ p['w_br_attn'][l]))
    x = x + g1 * (m @ p['w_o'][l])
    h2 = _rmsnorm(x, p['norm2_g'][l]) * (1 + sc2) + sh2
    if l % 2 == 0:
        i = l // 2
        f = _swiglu(h2, p['w_ff_gate'][i], p['w_ff_up'][i], p['w_ff_down'][i])
    else:
        i = l // 2
        f = _moe(h2, p['w_router'][i], p['w_moe_gate'][i], p['w_moe_up'][i], p['w_moe_down'][i])
    x = x + g2 * f
    return x, k.reshape(b, s, N_KV_HEADS, 2 * HEAD_DIM), v, conv_state


def setup_inputs(seed: int = 0) -> dict:
    key = jax.random.key(seed)
    ks = jax.random.split(key, 30)
    n_pages = PAST_LEN // PAGE_SIZE
    n_used = DEC_BATCH * n_pages
    n_pool = n_used + n_used // 4

    def nrm(i, shape, scale):
        return jax.random.normal(ks[i], shape, jnp.float32) * scale

    def gain(i, shape):
        return 1.0 + nrm(i, shape, 0.02)

    page_table = jax.random.permutation(ks[5], n_pool)[:n_used].reshape(DEC_BATCH, n_pages).astype(jnp.int32)
    return {
        'x_prompt': nrm(0, (BATCH, SEQ, D_MODEL), 1.0),
        'x_sample': nrm(1, (DEC_BATCH, DEC_SEQ, D_MODEL), 1.0),
        'cache_k': nrm(2, (DEPTH, n_pool, PAGE_SIZE, N_KV_HEADS, 2 * HEAD_DIM), 1.0),
        'cache_v': nrm(3, (DEPTH, n_pool, PAGE_SIZE, N_KV_HEADS, V_DIM), 1.0),
        'state_conv': nrm(4, (DEPTH, DEC_BATCH, CONV_WIDTH - 1, D_CONV), 1.0),
        'page_table': page_table,
        'c_prompt': nrm(6, (BATCH, D_MODEL), 1.0),
        'c_sample': nrm(7, (DEC_BATCH, D_MODEL), 1.0),
        'w_ada': nrm(8, (DEPTH, D_MODEL, 6 * D_MODEL), 0.5 * D_MODEL ** -0.5),
        'b_ada': nrm(9, (DEPTH, 6 * D_MODEL), 0.02),
        'norm1_g': gain(10, (DEPTH, D_MODEL)),
        'norm2_g': gain(11, (DEPTH, D_MODEL)),
        'w_in': nrm(12, (DEPTH, D_MODEL, IN_WIDTH), D_MODEL ** -0.5),
        'conv_w': nrm(13, (DEPTH, CONV_WIDTH, D_CONV), CONV_WIDTH ** -0.5),
        'lambda_q1': nrm(14, (DEPTH, HEAD_DIM), 0.1),
        'lambda_k1': nrm(15, (DEPTH, HEAD_DIM), 0.1),
        'lambda_q2': nrm(16, (DEPTH, HEAD_DIM), 0.1),
        'lambda_k2': nrm(17, (DEPTH, HEAD_DIM), 0.1),
        'subln_g': gain(18, (DEPTH, V_DIM)),
        'w_br_conv': nrm(19, (DEPTH, D_CONV, D_MODEL), D_CONV ** -0.5),
        'w_br_attn': nrm(20, (DEPTH, ATT_WIDTH, D_MODEL), ATT_WIDTH ** -0.5),
        'w_o': nrm(21, (DEPTH, D_MODEL, D_MODEL), D_MODEL ** -0.5),
        'w_ff_gate': nrm(22, (N_DENSE, D_MODEL, D_FF), D_MODEL ** -0.5),
        'w_ff_up': nrm(23, (N_DENSE, D_MODEL, D_FF), D_MODEL ** -0.5),
        'w_ff_down': nrm(24, (N_DENSE, D_FF, D_MODEL), D_FF ** -0.5),
        'w_router': nrm(25, (N_MOE, D_MODEL, N_EXPERTS), D_MODEL ** -0.5),
        'w_moe_gate': nrm(26, (N_MOE, N_EXPERTS, D_MODEL, D_FF_EXPERT), D_MODEL ** -0.5),
        'w_moe_up': nrm(27, (N_MOE, N_EXPERTS, D_MODEL, D_FF_EXPERT), D_MODEL ** -0.5),
        'w_moe_down': nrm(28, (N_MOE, N_EXPERTS, D_FF_EXPERT, D_MODEL), D_FF_EXPERT ** -0.5),
        'final_g': gain(29, (D_MODEL,)),
    }


def reference(x_prompt, x_sample, cache_k, cache_v, state_conv, page_table, c_prompt, c_sample,
              w_ada, b_ada, norm1_g, norm2_g, w_in, conv_w, lambda_q1, lambda_k1, lambda_q2, lambda_k2,
              subln_g, w_br_conv, w_br_attn, w_o, w_ff_gate, w_ff_up, w_ff_down, w_router,
              w_moe_gate, w_moe_up, w_moe_down, final_g):
    p = {'w_ada': w_ada, 'b_ada': b_ada, 'norm1_g': norm1_g, 'norm2_g': norm2_g, 'w_in': w_in,
         'conv_w': conv_w, 'lambda_q1': lambda_q1, 'lambda_k1': lambda_k1, 'lambda_q2': lambda_q2,
         'lambda_k2': lambda_k2, 'subln_g': subln_g, 'w_br_conv': w_br_conv, 'w_br_attn': w_br_attn,
         'w_o': w_o, 'w_ff_gate': w_ff_gate, 'w_ff_up': w_ff_up, 'w_ff_down': w_ff_down,
         'w_router': w_router, 'w_moe_gate': w_moe_gate, 'w_moe_up': w_moe_up, 'w_moe_down': w_moe_down}
    n_s = x_sample.shape[0]
    pos_p = jnp.arange(x_prompt.shape[1])
    pos_s = PAST_LEN + jnp.arange(x_sample.shape[1])

    def attend_prompt(q, k, v, l, lam):
        return _diff_attn(q, k, v, pos_p, pos_p, lam)

    def conv_prompt(u, w, l):
        pad = jnp.zeros((u.shape[0], CONV_WIDTH - 1, u.shape[2]), u.dtype)
        return _causal_conv(jnp.concatenate([pad, u], axis=1), w)

    def attend_sample(q, k, v, l, lam):
        kp = cache_k[l][page_table].reshape(n_s, -1, N_KV_HEADS, 2, HEAD_DIM).astype(k.dtype)
        vp = cache_v[l][page_table].reshape(n_s, -1, N_KV_HEADS, V_DIM).astype(v.dtype)
        k_all = jnp.concatenate([kp, k], axis=1)
        v_all = jnp.concatenate([vp, v], axis=1)
        k_pos = jnp.arange(k_all.shape[1])
        return _diff_attn(q, k_all, v_all, pos_s, k_pos, lam)

    def conv_sample(u, w, l):
        return _causal_conv(jnp.concatenate([state_conv[l].astype(u.dtype), u], axis=1), w)

    xp, xs = x_prompt, x_sample
    kps, vps, cps, kss, vss, css = [], [], [], [], [], []
    for l in range(DEPTH):
        xp, kp, vp, cp = _layer(xp, c_prompt, l, p, pos_p, attend_prompt, conv_prompt)
        xs, ksn, vsn, csn = _layer(xs, c_sample, l, p, pos_s, attend_sample, conv_sample)
        kps.append(kp); vps.append(vp); cps.append(cp)
        kss.append(ksn); vss.append(vsn); css.append(csn)
    y_prompt = _rmsnorm(xp, final_g)
    y_sample = _rmsnorm(xs, final_g)
    return (y_prompt, y_sample, jnp.stack(kps), jnp.stack(vps), jnp.stack(cps),
            jnp.stack(kss), jnp.stack(vss), jnp.stack(css))
```

```python
import functools
import math

import jax
import jax.numpy as jnp
from jax import lax
from jax.experimental import pallas as pl
from jax.experimental.pallas import tpu as pltpu

F32 = jnp.float32
BF16 = jnp.bfloat16

N_HEADS = 8
N_KV_HEADS = 4
GROUP = N_HEADS // N_KV_HEADS
HEAD_DIM = 64
V_DIM = 2 * HEAD_DIM
ROT_DIM = HEAD_DIM // 4
ROPE_THETA = 500000.0
TOP_K = 2
EPS = 1e-6

LANES = 128
V7X_VMEM_BYTES = 64 * 2 ** 20
VMEM_LIMIT = V7X_VMEM_BYTES - 8 * 2 ** 20

NEG_INF = float("-inf")


def _cparams(sem):
    return pltpu.CompilerParams(dimension_semantics=sem, vmem_limit_bytes=VMEM_LIMIT)


def _resident(shape):
    nd = len(shape)
    return pl.BlockSpec(shape, lambda *_: (0,) * nd, pipeline_mode=pl.Buffered(1))


def _mod_spec(rows, tiles_per_group, piece, d):
    return pl.BlockSpec((None, rows, d), lambda i: (i // tiles_per_group, 0, piece))


def _rmsnorm(x, g):
    return x * lax.rsqrt(jnp.mean(x * x, axis=-1, keepdims=True) + EPS) * g


def _ada_kernel(c_ref, w_ref, b_ref, o_ref):
    c = c_ref[...]
    a = (c * jax.nn.sigmoid(c)).astype(BF16)
    o_ref[...] = jnp.dot(a, w_ref[...].astype(BF16), preferred_element_type=F32) + b_ref[...]


def _ada(c_all, w_ada, b_ada):
    depth, d, d6 = w_ada.shape
    n = c_all.shape[0]
    return pl.pallas_call(
        _ada_kernel,
        grid=(depth, d6 // d),
        in_specs=[pl.BlockSpec((n, d), lambda l, j: (0, 0)),
                  pl.BlockSpec((None, d, d), lambda l, j: (l, 0, j)),
                  pl.BlockSpec((None, 1, d), lambda l, j: (l, 0, j))],
        out_specs=pl.BlockSpec((None, n, d), lambda l, j: (l, 0, j)),
        out_shape=jax.ShapeDtypeStruct((depth, n, d6), F32),
        compiler_params=_cparams(("arbitrary", "arbitrary")),
        name="ada",
    )(c_all, w_ada, b_ada.reshape(depth, 1, d6))


def _rope_chunk(xc, cos, sa, sb):
    half = ROT_DIM // 2
    return xc * cos + pltpu.roll(xc, LANES - half, 1) * sa + pltpu.roll(xc, half, 1) * sb


def _inproj_kernel(*refs, tm, d, sample, tiles_per_seq):
    if sample:
        (x_ref, sh_ref, sc_ref, ng_ref, win_ref, wbr_ref, cw_ref, cos_ref, sa_ref, sb_ref, st0_ref, st1_ref,
         mconv_ref, gate_ref, q_ref, kf_ref, vf_ref, kb_ref, vb_ref, cs_ref, ubuf) = refs
    else:
        (x_ref, sh_ref, sc_ref, ng_ref, win_ref, wbr_ref, cw_ref, cos_ref, sa_ref, sb_ref,
         mconv_ref, gate_ref, q_ref, kf_ref, vf_ref, kb_ref, vb_ref, cs_ref, ubuf) = refs
    i = pl.program_id(0)
    kvw = N_KV_HEADS * V_DIM
    o_q = 3 * d
    o_k = o_q + N_HEADS * 2 * HEAD_DIM
    o_v = o_k + kvw
    o_ga = o_v + kvw
    o_gb = o_ga + d

    h = (_rmsnorm(x_ref[...], ng_ref[...]) * (1.0 + sc_ref[...]) + sh_ref[...]).astype(BF16)

    def proj(a, b):
        return jnp.dot(h, win_ref[:, a:b], preferred_element_type=F32)

    u = proj(d, 2 * d) * proj(2 * d, 3 * d)
    if sample:
        ubuf[0:8, :] = jnp.zeros((8, d), F32)
    else:
        @pl.when(i % tiles_per_seq == 0)
        def _():
            ubuf[0:8, :] = jnp.zeros((8, d), F32)
    ubuf[8:tm + 8, :] = u
    up1 = ubuf[7:tm + 7, :]
    up2 = ubuf[6:tm + 6, :]
    if sample:
        t = lax.broadcasted_iota(jnp.int32, (tm, d), 0) & 7
        st0 = st0_ref[...]
        st1 = st1_ref[...]
        up1 = jnp.where(t >= 1, up1, st1)
        up2 = jnp.where(t >= 2, up2, jnp.where(t == 1, st1, st0))
        cs_ref[...] = u
    else:
        cs_ref[...] = ubuf[tm + 6:tm + 8, :]
        ubuf[0:8, :] = ubuf[tm:tm + 8, :]
    cu = cw_ref[0:1, :] * up2 + cw_ref[1:2, :] * up1 + cw_ref[2:3, :] * u
    y_conv = (proj(0, d) * cu).astype(BF16)
    yc = jnp.dot(y_conv, wbr_ref[...], preferred_element_type=F32)
    mconv_ref[...] = (jax.nn.sigmoid(proj(o_ga, o_gb)) * yc).astype(BF16)
    gate_ref[...] = jax.nn.sigmoid(proj(o_gb, o_gb + d)).astype(BF16)

    cos = cos_ref[...]
    sa = sa_ref[...]
    sb = sb_ref[...]
    lane = lax.broadcasted_iota(jnp.int32, (tm, LANES), 1)
    zq = proj(o_q, o_k) * (HEAD_DIM ** -0.5)
    for hq in range(N_HEADS):
        r = _rope_chunk(zq[:, hq * LANES:(hq + 1) * LANES], cos, sa, sb)
        kv, g = hq // GROUP, hq % GROUP
        q_ref[kv, 2 * g] = jnp.where(lane < HEAD_DIM, r, 0.0).astype(BF16)
        q_ref[kv, 2 * g + 1] = jnp.where(lane >= HEAD_DIM, r, 0.0).astype(BF16)
    zk = proj(o_k, o_v)
    for kv in range(N_KV_HEADS):
        r = _rope_chunk(zk[:, kv * LANES:(kv + 1) * LANES], cos, sa, sb)
        kf_ref[:, kv * LANES:(kv + 1) * LANES] = r
        kb_ref[:, kv * LANES:(kv + 1) * LANES] = r.astype(BF16)
    zv = proj(o_v, o_ga)
    vf_ref[...] = zv
    vb_ref[...] = zv.astype(BF16)


def _inproj(x, mod, ng, w_in, w_br, cw, tabs, state, *, tm, seq_len, sample):
    n, d = x.shape
    nt = n // tm
    kvw = N_KV_HEADS * V_DIM
    tiles_per_seq = max(seq_len // tm, 1)
    table_tiles = tabs[0].shape[0] // tm
    rows = mod.shape[1]
    row = lambda i: (i, 0)
    in_specs = [pl.BlockSpec((tm, d), row),
                _mod_spec(rows, tiles_per_seq, 0, d), _mod_spec(rows, tiles_per_seq, 1, d),
                _resident((1, d)), _resident(w_in.shape), _resident(w_br.shape), _resident(cw.shape)]
    in_specs += [pl.BlockSpec((tm, LANES), lambda i: (i % table_tiles, 0))] * 3
    args = [x, mod, mod, ng, w_in, w_br, cw, *tabs]
    if sample:
        in_specs += [pl.BlockSpec((tm, d), row)] * 2
        args += list(state)
        cs_shape, cs_spec = (n, d), pl.BlockSpec((tm, d), row)
    else:
        nseq = n // seq_len
        cs_shape, cs_spec = (nseq, 2, d), pl.BlockSpec((None, 2, d), lambda i: (i // tiles_per_seq, 0, 0))
    out_shape = [jax.ShapeDtypeStruct((n, d), BF16), jax.ShapeDtypeStruct((n, d), BF16),
                 jax.ShapeDtypeStruct((N_KV_HEADS, 2 * GROUP, n, LANES), BF16),
                 jax.ShapeDtypeStruct((n, kvw), F32), jax.ShapeDtypeStruct((n, kvw), F32),
                 jax.ShapeDtypeStruct((n, kvw), BF16), jax.ShapeDtypeStruct((n, kvw), BF16),
                 jax.ShapeDtypeStruct(cs_shape, F32)]
    out_specs = [pl.BlockSpec((tm, d), row), pl.BlockSpec((tm, d), row),
                 pl.BlockSpec((N_KV_HEADS, 2 * GROUP, tm, LANES), lambda i: (0, 0, i, 0)),
                 pl.BlockSpec((tm, kvw), row), pl.BlockSpec((tm, kvw), row),
                 pl.BlockSpec((tm, kvw), row), pl.BlockSpec((tm, kvw), row), cs_spec]
    return pl.pallas_call(
        functools.partial(_inproj_kernel, tm=tm, d=d, sample=sample, tiles_per_seq=tiles_per_seq),
        grid=(nt,), in_specs=in_specs, out_specs=out_specs, out_shape=out_shape,
        scratch_shapes=[pltpu.VMEM((tm + 8, d), F32)],
        compiler_params=_cparams(("arbitrary",)),
        name="inproj_sample" if sample else "inproj_prompt",
    )(*args)


def _lambda(lam_ref, lam_init):
    a = jnp.sum(lam_ref[0:1, :] * lam_ref[1:2, :], axis=-1, keepdims=True)
    b = jnp.sum(lam_ref[2:3, :] * lam_ref[3:4, :], axis=-1, keepdims=True)
    return jnp.exp(a) - jnp.exp(b) + lam_init


def _softmax_update(s, v, m_sc, l_sc, acc_sc):
    m_prev = m_sc[...]
    m_new = jnp.maximum(m_prev, jnp.max(s, axis=-1, keepdims=True))
    alpha = jnp.exp(m_prev - m_new)
    p = jnp.exp(s - m_new)
    l_sc[...] = alpha * l_sc[...] + jnp.sum(p, axis=-1, keepdims=True)
    acc_sc[...] = alpha * acc_sc[...] + jnp.dot(p.astype(BF16), v, preferred_element_type=F32)
    m_sc[...] = m_new


def _subln(o, g, lam_init):
    return _rmsnorm(o, g) * (1.0 - lam_init)


def _attn_prompt_kernel(q_ref, k_ref, v_ref, lam_ref, g_ref, o_ref, m_sc, l_sc, acc_sc, *, tq, lam_init):
    qi = pl.program_id(2)
    nrow = 2 * GROUP * tq
    q = q_ref[...].reshape(nrow, LANES)
    m_sc[...] = jnp.full(m_sc.shape, NEG_INF, F32)
    l_sc[...] = jnp.zeros(l_sc.shape, F32)
    acc_sc[...] = jnp.zeros(acc_sc.shape, F32)

    def step(ki, masked):
        start = pl.multiple_of(ki * tq, tq)
        k = k_ref[pl.ds(start, tq), :]
        v = v_ref[pl.ds(start, tq), :]
        s = lax.dot_general(q, k, (((1,), (1,)), ((), ())), preferred_element_type=F32)
        if masked:
            row = lax.broadcasted_iota(jnp.int32, (nrow, tq), 0) & (tq - 1)
            col = lax.broadcasted_iota(jnp.int32, (nrow, tq), 1)
            s = jnp.where(col <= row, s, NEG_INF)
        _softmax_update(s, v, m_sc, l_sc, acc_sc)

    def body(ki, carry):
        step(ki, False)
        return carry

    lax.fori_loop(0, qi, body, 0)
    step(qi, True)

    lam = _lambda(lam_ref, lam_init)
    o = acc_sc[...] / l_sc[...]
    for g in range(GROUP):
        og = o[(2 * g) * tq:(2 * g + 1) * tq] - lam * o[(2 * g + 1) * tq:(2 * g + 2) * tq]
        o_ref[:, g * V_DIM:(g + 1) * V_DIM] = _subln(og, g_ref[...], lam_init).astype(BF16)


def _attn_prompt(q, kb, vb, lam_vecs, subln_g, *, nseq, seq_len, tq, lam_init):
    n = kb.shape[0]
    nq = seq_len // tq
    nrow = 2 * GROUP * tq
    assert tq & (tq - 1) == 0 and seq_len % tq == 0
    return pl.pallas_call(
        functools.partial(_attn_prompt_kernel, tq=tq, lam_init=lam_init),
        grid=(nseq, N_KV_HEADS, nq),
        in_specs=[pl.BlockSpec((None, 2 * GROUP, tq, LANES), lambda b, h, i: (h, 0, b * nq + i, 0)),
                  pl.BlockSpec((seq_len, V_DIM), lambda b, h, i: (b, h)),
                  pl.BlockSpec((seq_len, V_DIM), lambda b, h, i: (b, h)),
                  pl.BlockSpec(lam_vecs.shape, lambda b, h, i: (0, 0)),
                  pl.BlockSpec(subln_g.shape, lambda b, h, i: (0, 0))],
        out_specs=pl.BlockSpec((tq, GROUP * V_DIM), lambda b, h, i: (b * nq + i, h)),
        out_shape=jax.ShapeDtypeStruct((n, N_HEADS * V_DIM), BF16),
        scratch_shapes=[pltpu.VMEM((nrow, 1), F32), pltpu.VMEM((nrow, 1), F32), pltpu.VMEM((nrow, V_DIM), F32)],
        compiler_params=_cparams(("arbitrary", "arbitrary", "arbitrary")),
        name="attn_prompt",
    )(q, kb, vb, lam_vecs, subln_g)


def _attn_sample_kernel(pt_ref, q_ref, kn_ref, vn_ref, lam_ref, g_ref, *refs, pages, dec_seq, lam_init):
    k_refs = refs[:pages]
    v_refs = refs[pages:2 * pages]
    o_ref, m_sc, l_sc, acc_sc = refs[2 * pages:]
    step = pl.program_id(1)
    q = q_ref[...]
    nrow = q.shape[0]
    rows_per_head = nrow // N_KV_HEADS

    @pl.when(step == 0)
    def _():
        m_sc[...] = jnp.full(m_sc.shape, NEG_INF, F32)
        l_sc[...] = jnp.zeros(l_sc.shape, F32)
        acc_sc[...] = jnp.zeros(acc_sc.shape, F32)

    k = jnp.concatenate([r[...].astype(BF16) for r in k_refs], axis=0)
    v = jnp.concatenate([r[...].astype(BF16) for r in v_refs], axis=0)
    s = lax.dot_general(q, k, (((1,), (1,)), ((), ())), preferred_element_type=F32)
    _softmax_update(s, v, m_sc, l_sc, acc_sc)

    @pl.when(step == pl.num_programs(1) - 1)
    def _():
        kn = kn_ref[...]
        sn = lax.dot_general(q, kn, (((1,), (1,)), ((), ())), preferred_element_type=F32)
        t = lax.broadcasted_iota(jnp.int32, sn.shape, 0) % dec_seq
        col = lax.broadcasted_iota(jnp.int32, sn.shape, 1)
        sn = jnp.where(col <= t, sn, NEG_INF)
        _softmax_update(sn, vn_ref[...], m_sc, l_sc, acc_sc)

        lam = _lambda(lam_ref, lam_init)
        half = rows_per_head // 2
        for h in range(N_KV_HEADS):
            r0 = h * rows_per_head
            blk = acc_sc[r0:r0 + rows_per_head, h * V_DIM:(h + 1) * V_DIM] / l_sc[r0:r0 + rows_per_head, :]
            o = blk[0:half] - lam * blk[half:rows_per_head]
            o_ref[h] = _subln(o, g_ref[...], lam_init)


def _attn_sample(page_table, qbd, kn, vn, lam_vecs, subln_g, cache_k, cache_v, *, layer, pages, dec_seq, lam_init):
    nb, n_pages = page_table.shape
    page = cache_k.shape[2]
    kvw = N_KV_HEADS * V_DIM
    nrow = qbd.shape[1]
    assert n_pages % pages == 0
    ck = cache_k.reshape(cache_k.shape[0], cache_k.shape[1], page, kvw)
    cv = cache_v.reshape(cache_v.shape[0], cache_v.shape[1], page, kvw)

    def page_spec(j):
        return pl.BlockSpec((None, None, page, kvw),
                            lambda b, s, pt: (layer, pt[b * n_pages + s * pages + j], 0, 0))

    fixed = lambda shape: pl.BlockSpec(shape, lambda b, s, pt: (0,) * len(shape))
    per_b = lambda shape: pl.BlockSpec((None,) + shape, lambda b, s, pt: (b,) + (0,) * len(shape))
    grid_spec = pltpu.PrefetchScalarGridSpec(
        num_scalar_prefetch=1,
        grid=(nb, n_pages // pages),
        in_specs=[per_b((nrow, kvw)), per_b((page, kvw)), per_b((page, kvw)),
                  fixed(lam_vecs.shape), fixed(subln_g.shape)]
                 + [page_spec(j) for j in range(pages)] * 2,
        out_specs=per_b((N_KV_HEADS, nrow // N_KV_HEADS // 2, V_DIM)),
        scratch_shapes=[pltpu.VMEM((nrow, 1), F32), pltpu.VMEM((nrow, 1), F32), pltpu.VMEM((nrow, kvw), F32)],
    )
    return pl.pallas_call(
        functools.partial(_attn_sample_kernel, pages=pages, dec_seq=dec_seq, lam_init=lam_init),
        grid_spec=grid_spec,
        out_shape=jax.ShapeDtypeStruct((nb, N_KV_HEADS, nrow // N_KV_HEADS // 2, V_DIM), F32),
        compiler_params=_cparams(("arbitrary", "arbitrary")),
        name="attn_sample",
    )(page_table.reshape(-1), qbd, kn, vn, lam_vecs, subln_g, *([ck] * pages), *([cv] * pages))


def _outproj_kernel(*refs, moe):
    if moe:
        (x_ref, mconv_ref, gate_ref, ya_ref, g1_ref, sh2_ref, sc2_ref, ng_ref, wba_ref, wo_ref, wr_ref,
         x1_ref, h2_ref, comb_ref) = refs
    else:
        (x_ref, mconv_ref, gate_ref, ya_ref, g1_ref, sh2_ref, sc2_ref, ng_ref, wba_ref, wo_ref,
         x1_ref, h2_ref) = refs
    att = jnp.dot(ya_ref[...], wba_ref[...], preferred_element_type=F32)
    m = mconv_ref[...].astype(F32) + gate_ref[...].astype(F32) * att
    x1 = x_ref[...] + g1_ref[...] * jnp.dot(m.astype(BF16), wo_ref[...], preferred_element_type=F32)
    x1_ref[...] = x1
    h2 = _rmsnorm(x1, ng_ref[...]) * (1.0 + sc2_ref[...]) + sh2_ref[...]
    h2_ref[...] = h2.astype(BF16)
    if moe:
        logits = jnp.dot(h2, wr_ref[...], preferred_element_type=F32, precision=lax.Precision.HIGHEST)
        ne = logits.shape[-1]
        idx = lax.broadcasted_iota(jnp.int32, logits.shape, 1)
        v1 = jnp.max(logits, axis=-1, keepdims=True)
        i1 = jnp.min(jnp.where(logits == v1, idx, ne), axis=-1, keepdims=True)
        rest = jnp.where(idx == i1, NEG_INF, logits)
        v2 = jnp.max(rest, axis=-1, keepdims=True)
        i2 = jnp.min(jnp.where(rest == v2, idx, ne), axis=-1, keepdims=True)
        e2 = jnp.exp(v2 - v1)
        den = 1.0 + e2
        comb_ref[...] = jnp.where(idx == i1, 1.0 / den, 0.0) + jnp.where(idx == i2, e2 / den, 0.0)


def _outproj(x, mconv, gate, ya, mod, ng, w_ba, w_o, w_router, *, tm, seq_len):
    n, d = x.shape
    tiles_per_seq = max(seq_len // tm, 1)
    rows = mod.shape[1]
    moe = w_router is not None
    row = lambda i: (i, 0)
    tile = pl.BlockSpec((tm, d), row)
    in_specs = [tile, tile, tile, tile,
                _mod_spec(rows, tiles_per_seq, 2, d), _mod_spec(rows, tiles_per_seq, 3, d),
                _mod_spec(rows, tiles_per_seq, 4, d), _resident((1, d)), _resident(w_ba.shape), _resident(w_o.shape)]
    args = [x, mconv, gate, ya, mod, mod, mod, ng, w_ba, w_o]
    out_shape = [jax.ShapeDtypeStruct((n, d), F32), jax.ShapeDtypeStruct((n, d), BF16)]
    out_specs = [tile, tile]
    if moe:
        ne = w_router.shape[1]
        in_specs.append(_resident(w_router.shape))
        args.append(w_router)
        out_shape.append(jax.ShapeDtypeStruct((n, ne), F32))
        out_specs.append(pl.BlockSpec((tm, ne), row))
    return pl.pallas_call(
        functools.partial(_outproj_kernel, moe=moe),
        grid=(n // tm,), in_specs=in_specs, out_specs=out_specs, out_shape=out_shape,
        compiler_params=_cparams(("arbitrary",)),
        name="outproj",
    )(*args)


def _ffn_kernel(*refs, moe, final):
    refs = list(refs)
    x1_ref, h2_ref, g2_ref = refs[:3]
    refs = refs[3:]
    comb_ref = refs.pop(0) if moe else None
    wg_ref, wu_ref, wd_ref = refs[:3]
    refs = refs[3:]
    fg_ref = refs.pop(0) if final else None
    o_ref, acc_sc = refs
    e = pl.program_id(1)
    j = pl.program_id(2)
    first = jnp.logical_and(e == 0, j == 0)
    last = jnp.logical_and(e == pl.num_programs(1) - 1, j == pl.num_programs(2) - 1)

    @pl.when(first)
    def _():
        acc_sc[...] = jnp.zeros(acc_sc.shape, F32)

    h = h2_ref[...]
    gate = jnp.dot(h, wg_ref[...], preferred_element_type=F32)
    up = jnp.dot(h, wu_ref[...], preferred_element_type=F32)
    act = (gate * jax.nn.sigmoid(gate) * up).astype(BF16)
    f = jnp.dot(act, wd_ref[...], preferred_element_type=F32)
    if moe:
        comb = comb_ref[...]
        idx = lax.broadcasted_iota(jnp.int32, comb.shape, 1)
        f = jnp.sum(jnp.where(idx == e, comb, 0.0), axis=-1, keepdims=True) * f
    acc_sc[...] += f

    @pl.when(last)
    def _():
        x2 = x1_ref[...] + g2_ref[...] * acc_sc[...]
        o_ref[...] = _rmsnorm(x2, fg_ref[...]) if final else x2


def _ffn(x1, h2, mod, comb, wg, wu, wd, final_g, *, tm, tf, seq_len):
    n, d = x1.shape
    ne, _, dff = wg.shape
    tiles_per_seq = max(seq_len // tm, 1)
    rows = mod.shape[1]
    moe = comb is not None
    final = final_g is not None
    row = lambda i, e, j: (i, 0)
    tile = pl.BlockSpec((tm, d), row)
    in_specs = [tile, tile, pl.BlockSpec((None, rows, d), lambda i, e, j: (i // tiles_per_seq, 0, 5))]
    args = [x1, h2, mod]
    if moe:
        in_specs.append(pl.BlockSpec((tm, ne), row))
        args.append(comb)
    in_specs += [pl.BlockSpec((None, d, tf), lambda i, e, j: (e, 0, j)),
                 pl.BlockSpec((None, d, tf), lambda i, e, j: (e, 0, j)),
                 pl.BlockSpec((None, tf, d), lambda i, e, j: (e, j, 0))]
    args += [wg, wu, wd]
    if final:
        in_specs.append(pl.BlockSpec((1, d), lambda i, e, j: (0, 0)))
        args.append(final_g)
    return pl.pallas_call(
        functools.partial(_ffn_kernel, moe=moe, final=final),
        grid=(n // tm, ne, dff // tf), in_specs=in_specs, out_specs=tile,
        out_shape=jax.ShapeDtypeStruct((n, d), F32),
        scratch_shapes=[pltpu.VMEM((tm, d), F32)],
        compiler_params=_cparams(("arbitrary", "arbitrary", "arbitrary")),
        name="moe" if moe else "ffn",
    )(*args)


def _rope_tables(pos):
    half = ROT_DIM // 2
    inv = ROPE_THETA ** (-jnp.arange(half, dtype=F32) / half)
    ang = pos.astype(F32)[:, None] * inv[None, :]
    cos, sin = jnp.cos(ang), jnp.sin(ang)
    r = jnp.arange(LANES) % HEAD_DIM
    idx = r % half
    lo = (r < half)[None, :]
    hi = ((r >= half) & (r < ROT_DIM))[None, :]
    c = jnp.where((r < ROT_DIM)[None, :], cos[:, idx], 1.0)
    sa = jnp.where(lo, -sin[:, idx], 0.0)
    sb = jnp.where(hi, sin[:, idx], 0.0)
    return c, sa, sb


def _pick_ff_tile(dff, cap):
    best = None
    for t in range(LANES, cap + 1, LANES):
        if dff % t == 0:
            best = t
    return best if best is not None else dff


def kernel(x_prompt, x_sample, cache_k, cache_v, state_conv, page_table, c_prompt, c_sample, w_ada, b_ada, norm1_g, norm2_g, w_in, conv_w, lambda_q1, lambda_k1, lambda_q2, lambda_k2, subln_g, w_br_conv, w_br_attn, w_o, w_ff_gate, w_ff_up, w_ff_down, w_router, w_moe_gate, w_moe_up, w_moe_down, final_g):
    nseq, seq_len, d = x_prompt.shape
    nb, dec_seq, _ = x_sample.shape
    depth = w_ada.shape[0]
    page = cache_k.shape[2]
    past_len = page_table.shape[1] * page
    kvw = N_KV_HEADS * V_DIM
    n_p, n_s = nseq * seq_len, nb * dec_seq
    tm_p = min(512, seq_len)
    tq = min(512, seq_len)
    assert dec_seq == 8 and d % LANES == 0

    mod_all = _ada(jnp.concatenate([c_prompt, c_sample], axis=0), w_ada, b_ada)
    tabs_p = _rope_tables(jnp.arange(seq_len))
    tabs_s = tuple(jnp.tile(t, (nb, 1)) for t in _rope_tables(past_len + jnp.arange(dec_seq)))
    final_row = final_g.reshape(1, d)
    eye_kv = jnp.eye(N_KV_HEADS, dtype=BF16)

    xp = x_prompt.reshape(n_p, d)
    xs = x_sample.reshape(n_s, d)
    outs = [[] for _ in range(6)]
    for l in range(depth):
        lam_init = 0.8 - 0.6 * math.exp(-0.3 * l)
        mod_p = mod_all[l, :nseq].reshape(nseq, 1, 6 * d)
        mod_s = jnp.repeat(mod_all[l, nseq:], dec_seq, axis=0).reshape(1, n_s, 6 * d)
        w_in_l = w_in[l].astype(BF16)
        w_brc = w_br_conv[l].astype(BF16)
        w_bra = w_br_attn[l].astype(BF16)
        w_o_l = w_o[l].astype(BF16)
        ng1 = norm1_g[l].reshape(1, d)
        ng2 = norm2_g[l].reshape(1, d)
        lam_vecs = jnp.stack([lambda_q1[l], lambda_k1[l], lambda_q2[l], lambda_k2[l]])
        sg = subln_g[l].reshape(1, V_DIM)
        st = state_conv[l]
        state = (jnp.repeat(st[:, 0], dec_seq, axis=0), jnp.repeat(st[:, 1], dec_seq, axis=0))
        moe = l % 2 == 1
        i = l // 2
        if moe:
            wr = w_router[i]
            wg, wu, wd = w_moe_gate[i].astype(BF16), w_moe_up[i].astype(BF16), w_moe_down[i].astype(BF16)
        else:
            wr = None
            wg, wu, wd = (w_ff_gate[i][None].astype(BF16), w_ff_up[i][None].astype(BF16),
                          w_ff_down[i][None].astype(BF16))
        tf = _pick_ff_tile(wg.shape[2], 1792)
        fin = final_row if l == depth - 1 else None

        mconv, gate, q, kf, vf, kb, vb, cs = _inproj(
            xp, mod_p, ng1, w_in_l, w_brc, conv_w[l], tabs_p, None, tm=tm_p, seq_len=seq_len, sample=False)
        ya = _attn_prompt(q, kb, vb, lam_vecs, sg, nseq=nseq, seq_len=seq_len, tq=tq, lam_init=lam_init)
        res = _outproj(xp, mconv, gate, ya, mod_p, ng2, w_bra, w_o_l, wr, tm=tm_p, seq_len=seq_len)
        comb = res[2] if moe else None
        xp = _ffn(res[0], res[1], mod_p, comb, wg, wu, wd, fin, tm=tm_p, tf=tf, seq_len=seq_len)
        outs[0].append(kf.reshape(nseq, seq_len, N_KV_HEADS, V_DIM))
        outs[1].append(vf.reshape(nseq, seq_len, N_KV_HEADS, V_DIM))
        outs[2].append(cs)

        mconv, gate, q, kf, vf, kb, vb, u = _inproj(
            xs, mod_s, ng1, w_in_l, w_brc, conv_w[l], tabs_s, state, tm=n_s, seq_len=n_s, sample=True)
        qe = q.reshape(N_KV_HEADS, GROUP, 2, nb, dec_seq, LANES).transpose(3, 0, 2, 1, 4, 5)
        qbd = (qe[:, :, :, :, :, None, :] * eye_kv[None, :, None, None, None, :, None]).reshape(
            nb, N_KV_HEADS * 2 * GROUP * dec_seq, kvw)
        pad = ((0, 0), (0, page - dec_seq), (0, 0))
        kn = jnp.pad(kb.reshape(nb, dec_seq, kvw), pad)
        vn = jnp.pad(vb.reshape(nb, dec_seq, kvw), pad)
        o = _attn_sample(page_table, qbd, kn, vn, lam_vecs, sg, cache_k, cache_v,
                         layer=l, pages=16, dec_seq=dec_seq, lam_init=lam_init)
        ya = o.reshape(nb, N_KV_HEADS, GROUP, dec_seq, V_DIM).transpose(0, 3, 1, 2, 4).reshape(n_s, d).astype(BF16)
        res = _outproj(xs, mconv, gate, ya, mod_s, ng2, w_bra, w_o_l, wr, tm=n_s, seq_len=n_s)
        comb = res[2] if moe else None
        xs = _ffn(res[0], res[1], mod_s, comb, wg, wu, wd, fin, tm=n_s, tf=tf, seq_len=n_s)
        outs[3].append(kf.reshape(nb, dec_seq, N_KV_HEADS, V_DIM))
        outs[4].append(vf.reshape(nb, dec_seq, N_KV_HEADS, V_DIM))
        outs[5].append(u.reshape(nb, dec_seq, d)[:, dec_seq - 2:])

    return (xp.reshape(nseq, seq_len, d), xs.reshape(nb, dec_seq, d),
            jnp.stack(outs[0]), jnp.stack(outs[1]), jnp.stack(outs[2]),
            jnp.stack(outs[3]), jnp.stack(outs[4]), jnp.stack(outs[5]))
```

```python
import functools
import math

import jax
import jax.numpy as jnp
from jax import lax
from jax.experimental import pallas as pl
from jax.experimental.pallas import tpu as pltpu

F32 = jnp.float32
BF16 = jnp.bfloat16

N_HEADS = 8
N_KV_HEADS = 4
GROUP = N_HEADS // N_KV_HEADS
HEAD_DIM = 64
V_DIM = 2 * HEAD_DIM
ROT_DIM = HEAD_DIM // 4
ROPE_THETA = 500000.0
TOP_K = 2
EPS = 1e-6

LANES = 128
V7X_VMEM_BYTES = 64 * 2 ** 20
VMEM_LIMIT = V7X_VMEM_BYTES - 8 * 2 ** 20

NEG_INF = float("-inf")
LOG2E = math.log2(math.e)


def _cparams(sem):
    return pltpu.CompilerParams(dimension_semantics=sem, vmem_limit_bytes=VMEM_LIMIT)


def _resident(shape):
    nd = len(shape)
    return pl.BlockSpec(shape, lambda *_: (0,) * nd, pipeline_mode=pl.Buffered(1))


def _mod_spec(rows, tiles_per_group, piece, d):
    return pl.BlockSpec((None, rows, d), lambda i: (i // tiles_per_group, 0, piece))


def _rmsnorm(x, g):
    return x * lax.rsqrt(jnp.mean(x * x, axis=-1, keepdims=True) + EPS) * g


def _ada_kernel(c_ref, w_ref, b_ref, o_ref):
    c = c_ref[...]
    a = (c * jax.nn.sigmoid(c)).astype(BF16)
    o_ref[...] = jnp.dot(a, w_ref[...].astype(BF16), preferred_element_type=F32) + b_ref[...]


def _ada(c_all, w_ada, b_ada):
    depth, d, d6 = w_ada.shape
    n = c_all.shape[0]
    return pl.pallas_call(
        _ada_kernel,
        grid=(depth, d6 // d),
        in_specs=[pl.BlockSpec((n, d), lambda l, j: (0, 0)),
                  pl.BlockSpec((None, d, d), lambda l, j: (l, 0, j)),
                  pl.BlockSpec((None, 1, d), lambda l, j: (l, 0, j))],
        out_specs=pl.BlockSpec((None, n, d), lambda l, j: (l, 0, j)),
        out_shape=jax.ShapeDtypeStruct((depth, n, d6), F32),
        compiler_params=_cparams(("arbitrary", "arbitrary")),
        name="ada",
    )(c_all, w_ada, b_ada.reshape(depth, 1, d6))


def _rope_chunk(xc, cos, sa, sb):
    half = ROT_DIM // 2
    return xc * cos + pltpu.roll(xc, LANES - half, 1) * sa + pltpu.roll(xc, half, 1) * sb


def _inproj_kernel(*refs, tm, d, sample, tiles_per_seq):
    if sample:
        (x_ref, sh_ref, sc_ref, ng_ref, win_ref, wbr_ref, cw_ref, cos_ref, sa_ref, sb_ref, st0_ref, st1_ref,
         mconv_ref, gate_ref, q_ref, kf_ref, vf_ref, kb_ref, vb_ref, cs_ref, ubuf) = refs
    else:
        (x_ref, sh_ref, sc_ref, ng_ref, win_ref, wbr_ref, cw_ref, cos_ref, sa_ref, sb_ref,
         mconv_ref, gate_ref, q_ref, kf_ref, vf_ref, kb_ref, vt_ref, cs_ref, ubuf) = refs
    i = pl.program_id(0)
    kvw = N_KV_HEADS * V_DIM
    o_q = 3 * d
    o_k = o_q + N_HEADS * 2 * HEAD_DIM
    o_v = o_k + kvw
    o_ga = o_v + kvw
    o_gb = o_ga + d

    h = (_rmsnorm(x_ref[...], ng_ref[...]) * (1.0 + sc_ref[...]) + sh_ref[...]).astype(BF16)

    def proj(a, b):
        return jnp.dot(h, win_ref[:, a:b], preferred_element_type=F32)

    u = proj(d, 2 * d) * proj(2 * d, 3 * d)
    if sample:
        ubuf[0:8, :] = jnp.zeros((8, d), F32)
    else:
        @pl.when(i % tiles_per_seq == 0)
        def _():
            ubuf[0:8, :] = jnp.zeros((8, d), F32)
    ubuf[8:tm + 8, :] = u
    up1 = ubuf[7:tm + 7, :]
    up2 = ubuf[6:tm + 6, :]
    if sample:
        t = lax.broadcasted_iota(jnp.int32, (tm, d), 0) & 7
        st0 = st0_ref[...]
        st1 = st1_ref[...]
        up1 = jnp.where(t >= 1, up1, st1)
        up2 = jnp.where(t >= 2, up2, jnp.where(t == 1, st1, st0))
        cs_ref[...] = u
    else:
        cs_ref[...] = ubuf[tm + 6:tm + 8, :]
        ubuf[0:8, :] = ubuf[tm:tm + 8, :]
    cu = cw_ref[0:1, :] * up2 + cw_ref[1:2, :] * up1 + cw_ref[2:3, :] * u
    y_conv = (proj(0, d) * cu).astype(BF16)
    yc = jnp.dot(y_conv, wbr_ref[...], preferred_element_type=F32)
    mconv_ref[...] = (jax.nn.sigmoid(proj(o_ga, o_gb)) * yc).astype(BF16)
    gate_ref[...] = jax.nn.sigmoid(proj(o_gb, o_gb + d)).astype(BF16)

    cos = cos_ref[...]
    sa = sa_ref[...]
    sb = sb_ref[...]
    lane = lax.broadcasted_iota(jnp.int32, (tm, LANES), 1)
    zq = proj(o_q, o_k) * (HEAD_DIM ** -0.5 * LOG2E)
    for hq in range(N_HEADS):
        r = _rope_chunk(zq[:, hq * LANES:(hq + 1) * LANES], cos, sa, sb)
        kv, g = hq // GROUP, hq % GROUP
        q_ref[kv, 2 * g] = jnp.where(lane < HEAD_DIM, r, 0.0).astype(BF16)
        q_ref[kv, 2 * g + 1] = jnp.where(lane >= HEAD_DIM, r, 0.0).astype(BF16)
    zk = proj(o_k, o_v)
    for kv in range(N_KV_HEADS):
        r = _rope_chunk(zk[:, kv * LANES:(kv + 1) * LANES], cos, sa, sb)
        kf_ref[:, kv * LANES:(kv + 1) * LANES] = r
        kb_ref[:, kv * LANES:(kv + 1) * LANES] = r.astype(BF16)
    zv = proj(o_v, o_ga)
    vf_ref[...] = zv
    if sample:
        vb_ref[...] = zv.astype(BF16)
    else:
        tk = vt_ref.shape[-1]
        for c in range(tm // tk):
            for kv in range(N_KV_HEADS):
                vt_ref[c, kv] = zv[c * tk:(c + 1) * tk, kv * V_DIM:(kv + 1) * V_DIM].T.astype(BF16)


def _inproj(x, mod, ng, w_in, w_br, cw, tabs, state, *, tm, seq_len, sample, tk=None):
    n, d = x.shape
    nt = n // tm
    kvw = N_KV_HEADS * V_DIM
    tiles_per_seq = max(seq_len // tm, 1)
    table_tiles = tabs[0].shape[0] // tm
    rows = mod.shape[1]
    row = lambda i: (i, 0)
    in_specs = [pl.BlockSpec((tm, d), row),
                _mod_spec(rows, tiles_per_seq, 0, d), _mod_spec(rows, tiles_per_seq, 1, d),
                _resident((1, d)), _resident(w_in.shape), _resident(w_br.shape), _resident(cw.shape)]
    in_specs += [pl.BlockSpec((tm, LANES), lambda i: (i % table_tiles, 0))] * 3
    args = [x, mod, mod, ng, w_in, w_br, cw, *tabs]
    if sample:
        in_specs += [pl.BlockSpec((tm, d), row)] * 2
        args += list(state)
        cs_shape, cs_spec = (n, d), pl.BlockSpec((tm, d), row)
        v_shape, v_spec = (n, kvw), pl.BlockSpec((tm, kvw), row)
    else:
        nseq = n // seq_len
        cs_shape, cs_spec = (nseq, 2, d), pl.BlockSpec((None, 2, d), lambda i: (i // tiles_per_seq, 0, 0))
        v_shape = (n // tk, N_KV_HEADS, V_DIM, tk)
        v_spec = pl.BlockSpec((tm // tk, N_KV_HEADS, V_DIM, tk), lambda i: (i, 0, 0, 0))
    out_shape = [jax.ShapeDtypeStruct((n, d), BF16), jax.ShapeDtypeStruct((n, d), BF16),
                 jax.ShapeDtypeStruct((N_KV_HEADS, 2 * GROUP, n, LANES), BF16),
                 jax.ShapeDtypeStruct((n, kvw), F32), jax.ShapeDtypeStruct((n, kvw), F32),
                 jax.ShapeDtypeStruct((n, kvw), BF16), jax.ShapeDtypeStruct(v_shape, BF16),
                 jax.ShapeDtypeStruct(cs_shape, F32)]
    out_specs = [pl.BlockSpec((tm, d), row), pl.BlockSpec((tm, d), row),
                 pl.BlockSpec((N_KV_HEADS, 2 * GROUP, tm, LANES), lambda i: (0, 0, i, 0)),
                 pl.BlockSpec((tm, kvw), row), pl.BlockSpec((tm, kvw), row),
                 pl.BlockSpec((tm, kvw), row), v_spec, cs_spec]
    return pl.pallas_call(
        functools.partial(_inproj_kernel, tm=tm, d=d, sample=sample, tiles_per_seq=tiles_per_seq),
        grid=(nt,), in_specs=in_specs, out_specs=out_specs, out_shape=out_shape,
        scratch_shapes=[pltpu.VMEM((tm + 8, d), F32)],
        compiler_params=_cparams(("arbitrary",)),
        name="inproj_sample" if sample else "inproj_prompt",
    )(*args)


def _lambda(lam_ref, lam_init):
    a = jnp.sum(lam_ref[0:1, :] * lam_ref[1:2, :], axis=-1, keepdims=True)
    b = jnp.sum(lam_ref[2:3, :] * lam_ref[3:4, :], axis=-1, keepdims=True)
    return jnp.exp(a) - jnp.exp(b) + lam_init


def _softmax_update(s, v, m_sc, l_sc, acc_sc):
    m_prev = m_sc[...]
    m_new = jnp.maximum(m_prev, jnp.max(s, axis=-1, keepdims=True))
    alpha = jnp.exp2(m_prev - m_new)
    p = jnp.exp2(s - m_new)
    l_sc[...] = alpha * l_sc[...] + jnp.sum(p, axis=-1, keepdims=True)
    acc_sc[...] = alpha * acc_sc[...] + jnp.dot(p.astype(BF16), v, preferred_element_type=F32)
    m_sc[...] = m_new


def _subln(o, g, lam_init):
    return _rmsnorm(o, g) * (1.0 - lam_init)


def _attn_prompt_kernel(q_ref, k_ref, vt_ref, lam_ref, g_ref, o_ref, m_sc, l_sc, acc_sc, *, tq, lam_init):
    qi = pl.program_id(2)
    nslab = 2 * GROUP
    tk = vt_ref.shape[-1]
    ratio = tq // tk
    m_sc[...] = jnp.full(m_sc.shape, NEG_INF, F32)
    l_sc[...] = jnp.zeros(l_sc.shape, F32)
    acc_sc[...] = jnp.zeros(acc_sc.shape, F32)

    def step(ki, diag):
        start = pl.multiple_of(ki * tk, tk)
        k = k_ref[pl.ds(start, tk), :]
        vt = vt_ref[ki]
        masked = diag is not None
        if masked:
            key = lax.broadcasted_iota(jnp.int32, (tk, tq), 0) + diag * tk
            tok = lax.broadcasted_iota(jnp.int32, (tk, tq), 1)
            keep = key <= tok
        for sl in range(nslab):
            st = lax.dot_general(k, q_ref[sl], (((1,), (1,)), ((), ())), preferred_element_type=F32)
            if masked:
                st = jnp.where(keep, st, NEG_INF)
            m_prev = m_sc[sl]
            m_new = jnp.maximum(m_prev, jnp.max(st, axis=0, keepdims=True))
            alpha = jnp.exp2(m_prev - m_new)
            p = jnp.exp2(st - m_new)
            l_sc[sl] = alpha * l_sc[sl] + jnp.sum(p, axis=0, keepdims=True)
            acc_sc[sl] = alpha * acc_sc[sl] + jnp.dot(vt, p.astype(BF16), preferred_element_type=F32)
            m_sc[sl] = m_new

    def body(ki, carry):
        step(ki, None)
        return carry

    lax.fori_loop(0, qi * ratio, body, 0)
    for dg in range(ratio):
        step(qi * ratio + dg, dg)

    lam = _lambda(lam_ref, lam_init)
    for g in range(GROUP):
        o1 = acc_sc[2 * g] * (1.0 / l_sc[2 * g])
        o2 = acc_sc[2 * g + 1] * (1.0 / l_sc[2 * g + 1])
        og = o1 - lam * o2
        ms = jnp.mean(og * og, axis=0, keepdims=True)
        y = og * lax.rsqrt(ms + EPS) * g_ref[...] * (1.0 - lam_init)
        o_ref[:, g * V_DIM:(g + 1) * V_DIM] = y.T.astype(BF16)


def _attn_prompt(q, kb, vt, lam_vecs, subln_col, *, nseq, seq_len, tq, lam_init):
    n = kb.shape[0]
    nq = seq_len // tq
    nslab = 2 * GROUP
    tk = vt.shape[-1]
    assert seq_len % tq == 0 and tq % tk == 0 and vt.shape == (n // tk, N_KV_HEADS, V_DIM, tk)
    return pl.pallas_call(
        functools.partial(_attn_prompt_kernel, tq=tq, lam_init=lam_init),
        grid=(nseq, N_KV_HEADS, nq),
        in_specs=[pl.BlockSpec((None, nslab, tq, LANES), lambda b, h, i: (h, 0, b * nq + i, 0)),
                  pl.BlockSpec((seq_len, V_DIM), lambda b, h, i: (b, h)),
                  pl.BlockSpec((seq_len // tk, None, V_DIM, tk), lambda b, h, i: (b, h, 0, 0)),
                  pl.BlockSpec(lam_vecs.shape, lambda b, h, i: (0, 0)),
                  pl.BlockSpec(subln_col.shape, lambda b, h, i: (0, 0))],
        out_specs=pl.BlockSpec((tq, GROUP * V_DIM), lambda b, h, i: (b * nq + i, h)),
        out_shape=jax.ShapeDtypeStruct((n, N_HEADS * V_DIM), BF16),
        scratch_shapes=[pltpu.VMEM((nslab, 1, tq), F32), pltpu.VMEM((nslab, 1, tq), F32),
                        pltpu.VMEM((nslab, V_DIM, tq), F32)],
        compiler_params=_cparams(("arbitrary", "arbitrary", "arbitrary")),
        name="attn_prompt",
    )(q, kb, vt, lam_vecs, subln_col)


def _attn_sample_kernel(pt_ref, q_ref, kn_ref, vn_ref, lam_ref, g_ref, *refs, pages, dec_seq, lam_init):
    k_refs = refs[:pages]
    v_refs = refs[pages:2 * pages]
    o_ref, m_sc, l_sc, acc_sc = refs[2 * pages:]
    step = pl.program_id(1)
    q = q_ref[...]
    nrow = q.shape[0]
    rows_per_head = nrow // N_KV_HEADS

    @pl.when(step == 0)
    def _():
        m_sc[...] = jnp.full(m_sc.shape, NEG_INF, F32)
        l_sc[...] = jnp.zeros(l_sc.shape, F32)
        acc_sc[...] = jnp.zeros(acc_sc.shape, F32)

    def page_rows(ref):
        page = ref.shape[0] // N_KV_HEADS
        return jnp.concatenate([ref[pl.ds(h, page, stride=N_KV_HEADS), :] for h in range(N_KV_HEADS)],
                               axis=1).astype(BF16)

    k = jnp.concatenate([page_rows(r) for r in k_refs], axis=0)
    v = jnp.concatenate([page_rows(r) for r in v_refs], axis=0)
    s = lax.dot_general(q, k, (((1,), (1,)), ((), ())), preferred_element_type=F32)
    _softmax_update(s, v, m_sc, l_sc, acc_sc)

    @pl.when(step == pl.num_programs(1) - 1)
    def _():
        kn = kn_ref[...]
        sn = lax.dot_general(q, kn, (((1,), (1,)), ((), ())), preferred_element_type=F32)
        t = lax.broadcasted_iota(jnp.int32, sn.shape, 0) % dec_seq
        col = lax.broadcasted_iota(jnp.int32, sn.shape, 1)
        sn = jnp.where(col <= t, sn, NEG_INF)
        _softmax_update(sn, vn_ref[...], m_sc, l_sc, acc_sc)

        lam = _lambda(lam_ref, lam_init)
        half = rows_per_head // 2
        for h in range(N_KV_HEADS):
            r0 = h * rows_per_head
            blk = acc_sc[r0:r0 + rows_per_head, h * V_DIM:(h + 1) * V_DIM] / l_sc[r0:r0 + rows_per_head, :]
            o = blk[0:half] - lam * blk[half:rows_per_head]
            o_ref[h] = _subln(o, g_ref[...], lam_init)


def _attn_sample(page_table, qbd, kn, vn, lam_vecs, subln_g, cache_k, cache_v, *, layer, pages, dec_seq, lam_init):
    nb, n_pages = page_table.shape
    page = cache_k.shape[2]
    kvw = N_KV_HEADS * V_DIM
    nrow = qbd.shape[1]
    assert n_pages % pages == 0
    ck = cache_k.reshape(cache_k.shape[0], cache_k.shape[1], page * N_KV_HEADS, V_DIM)
    cv = cache_v.reshape(cache_v.shape[0], cache_v.shape[1], page * N_KV_HEADS, V_DIM)

    def page_spec(j):
        return pl.BlockSpec((None, None, page * N_KV_HEADS, V_DIM),
                            lambda b, s, pt: (layer, pt[b * n_pages + s * pages + j], 0, 0))

    fixed = lambda shape: pl.BlockSpec(shape, lambda b, s, pt: (0,) * len(shape))
    per_b = lambda shape: pl.BlockSpec((None,) + shape, lambda b, s, pt: (b,) + (0,) * len(shape))
    grid_spec = pltpu.PrefetchScalarGridSpec(
        num_scalar_prefetch=1,
        grid=(nb, n_pages // pages),
        in_specs=[per_b((nrow, kvw)), per_b((page, kvw)), per_b((page, kvw)),
                  fixed(lam_vecs.shape), fixed(subln_g.shape)]
                 + [page_spec(j) for j in range(pages)] * 2,
        out_specs=per_b((N_KV_HEADS, nrow // N_KV_HEADS // 2, V_DIM)),
        scratch_shapes=[pltpu.VMEM((nrow, 1), F32), pltpu.VMEM((nrow, 1), F32), pltpu.VMEM((nrow, kvw), F32)],
    )
    return pl.pallas_call(
        functools.partial(_attn_sample_kernel, pages=pages, dec_seq=dec_seq, lam_init=lam_init),
        grid_spec=grid_spec,
        out_shape=jax.ShapeDtypeStruct((nb, N_KV_HEADS, nrow // N_KV_HEADS // 2, V_DIM), F32),
        compiler_params=_cparams(("arbitrary", "arbitrary")),
        name="attn_sample",
    )(page_table.reshape(-1), qbd, kn, vn, lam_vecs, subln_g, *([ck] * pages), *([cv] * pages))


def _outproj_kernel(*refs, moe):
    if moe:
        (x_ref, mconv_ref, gate_ref, ya_ref, g1_ref, sh2_ref, sc2_ref, ng_ref, wba_ref, wo_ref, wr_ref,
         x1_ref, h2_ref, comb_ref) = refs
    else:
        (x_ref, mconv_ref, gate_ref, ya_ref, g1_ref, sh2_ref, sc2_ref, ng_ref, wba_ref, wo_ref,
         x1_ref, h2_ref) = refs
    att = jnp.dot(ya_ref[...], wba_ref[...], preferred_element_type=F32)
    m = mconv_ref[...].astype(F32) + gate_ref[...].astype(F32) * att
    x1 = x_ref[...] + g1_ref[...] * jnp.dot(m.astype(BF16), wo_ref[...], preferred_element_type=F32)
    x1_ref[...] = x1
    h2 = _rmsnorm(x1, ng_ref[...]) * (1.0 + sc2_ref[...]) + sh2_ref[...]
    h2_ref[...] = h2.astype(BF16)
    if moe:
        logits = lax.dot_general(wr_ref[...], h2, (((1,), (1,)), ((), ())), preferred_element_type=F32,
                                 precision=lax.Precision.HIGHEST)
        ne = logits.shape[0]
        idx = lax.broadcasted_iota(jnp.int32, logits.shape, 0)
        v1 = jnp.max(logits, axis=0, keepdims=True)
        i1 = jnp.min(jnp.where(logits == v1, idx, ne), axis=0, keepdims=True)
        rest = jnp.where(idx == i1, NEG_INF, logits)
        v2 = jnp.max(rest, axis=0, keepdims=True)
        i2 = jnp.min(jnp.where(rest == v2, idx, ne), axis=0, keepdims=True)
        e2 = jnp.exp(v2 - v1)
        den = 1.0 + e2
        comb_ref[...] = jnp.where(idx == i1, 1.0 / den, 0.0) + jnp.where(idx == i2, e2 / den, 0.0)


def _outproj(x, mconv, gate, ya, mod, ng, w_ba, w_o, w_router, *, tm, seq_len):
    n, d = x.shape
    tiles_per_seq = max(seq_len // tm, 1)
    rows = mod.shape[1]
    moe = w_router is not None
    row = lambda i: (i, 0)
    tile = pl.BlockSpec((tm, d), row)
    in_specs = [tile, tile, tile, tile,
                _mod_spec(rows, tiles_per_seq, 2, d), _mod_spec(rows, tiles_per_seq, 3, d),
                _mod_spec(rows, tiles_per_seq, 4, d), _resident((1, d)), _resident(w_ba.shape), _resident(w_o.shape)]
    args = [x, mconv, gate, ya, mod, mod, mod, ng, w_ba, w_o]
    out_shape = [jax.ShapeDtypeStruct((n, d), F32), jax.ShapeDtypeStruct((n, d), BF16)]
    out_specs = [tile, tile]
    if moe:
        ne = w_router.shape[0]
        in_specs.append(_resident(w_router.shape))
        args.append(w_router)
        out_shape.append(jax.ShapeDtypeStruct((ne, n), F32))
        out_specs.append(pl.BlockSpec((ne, tm), lambda i: (0, i)))
    return pl.pallas_call(
        functools.partial(_outproj_kernel, moe=moe),
        grid=(n // tm,), in_specs=in_specs, out_specs=out_specs, out_shape=out_shape,
        compiler_params=_cparams(("arbitrary",)),
        name="outproj",
    )(*args)


def _swiglu(x, wg_ref, wu_ref, wd_ref):
    gate = jnp.dot(x, wg_ref[...], preferred_element_type=F32)
    up = jnp.dot(x, wu_ref[...], preferred_element_type=F32)
    act = (gate * jax.nn.sigmoid(gate) * up).astype(BF16)
    return jnp.dot(act, wd_ref[...], preferred_element_type=F32)


def _residual_out(x1_ref, g2_ref, f, fg_ref):
    x2 = x1_ref[...] + g2_ref[...] * f
    return x2 if fg_ref is None else _rmsnorm(x2, fg_ref[...])


def _ffn_kernel(*refs, final):
    if final:
        x1_ref, h2_ref, g2_ref, wg_ref, wu_ref, wd_ref, fg_ref, o_ref, acc_sc = refs
    else:
        x1_ref, h2_ref, g2_ref, wg_ref, wu_ref, wd_ref, o_ref, acc_sc = refs
        fg_ref = None
    j = pl.program_id(1)

    @pl.when(j == 0)
    def _():
        acc_sc[...] = jnp.zeros(acc_sc.shape, F32)

    acc_sc[...] += _swiglu(h2_ref[...], wg_ref, wu_ref, wd_ref)

    @pl.when(j == pl.num_programs(1) - 1)
    def _():
        o_ref[...] = _residual_out(x1_ref, g2_ref, acc_sc[...], fg_ref)


def _ffn(x1, h2, mod, wg, wu, wd, final_g, *, tm, tf, seq_len):
    n, d = x1.shape
    dff = wg.shape[1]
    tiles_per_seq = max(seq_len // tm, 1)
    rows = mod.shape[1]
    final = final_g is not None
    tile = pl.BlockSpec((tm, d), lambda i, j: (i, 0))
    in_specs = [tile, tile, pl.BlockSpec((None, rows, d), lambda i, j: (i // tiles_per_seq, 0, 5)),
                pl.BlockSpec((d, tf), lambda i, j: (0, j)), pl.BlockSpec((d, tf), lambda i, j: (0, j)),
                pl.BlockSpec((tf, d), lambda i, j: (j, 0))]
    args = [x1, h2, mod, wg, wu, wd]
    if final:
        in_specs.append(pl.BlockSpec((1, d), lambda i, j: (0, 0)))
        args.append(final_g)
    return pl.pallas_call(
        functools.partial(_ffn_kernel, final=final),
        grid=(n // tm, dff // tf), in_specs=in_specs, out_specs=tile,
        out_shape=jax.ShapeDtypeStruct((n, d), F32),
        scratch_shapes=[pltpu.VMEM((tm, d), F32)],
        compiler_params=_cparams(("arbitrary", "arbitrary")),
        name="ffn",
    )(*args)


MOE_CHUNK = 128


def _moe_kernel(*refs, final, nj):
    if final:
        (x1_ref, h2_ref, g2_ref, comb_ref, tri_ref, wg_ref, wu_ref, wd_ref, fg_ref,
         o_ref, rank_sc, xc_sc, y_sc, nch_sm) = refs
    else:
        (x1_ref, h2_ref, g2_ref, comb_ref, tri_ref, wg_ref, wu_ref, wd_ref,
         o_ref, rank_sc, xc_sc, y_sc, nch_sm) = refs
        fg_ref = None
    e = pl.program_id(1)
    j = pl.program_id(2)
    ne = pl.num_programs(1)
    tb = h2_ref.shape[0]

    @pl.when(jnp.logical_and(e == 0, j == 0))
    def _():
        routed = comb_ref[...] > 0.0
        cnt = jnp.dot(jnp.where(routed, 1.0, 0.0).astype(BF16), tri_ref[...], preferred_element_type=F32)
        rank_sc[...] = jnp.where(routed, cnt, -1.0)
        o_ref[...] = jnp.zeros(o_ref.shape, F32)

    rank = rank_sc[pl.ds(e, 1), :]

    def one_hot(c):
        slot = (lax.broadcasted_iota(jnp.int32, (MOE_CHUNK, tb), 0) + c * MOE_CHUNK).astype(F32)
        return rank == slot

    @pl.when(j == 0)
    def _():
        n_tok = jnp.sum(jnp.where(rank >= 0.0, 1.0, 0.0)).astype(jnp.int32)
        nch = (n_tok + (MOE_CHUNK - 1)) // MOE_CHUNK
        nch_sm[0] = nch

        def body(c, carry):
            r0 = pl.multiple_of(c * MOE_CHUNK, MOE_CHUNK)
            sel = jnp.where(one_hot(c), 1.0, 0.0).astype(BF16)
            xc = jnp.dot(sel, h2_ref[...], preferred_element_type=F32).astype(BF16)
            xc_sc[pl.ds(r0, MOE_CHUNK), :] = xc
            y_sc[pl.ds(r0, MOE_CHUNK), :] = _swiglu(xc, wg_ref, wu_ref, wd_ref)
            return carry

        lax.fori_loop(0, nch, body, 0)

    if nj > 2:
        @pl.when(jnp.logical_and(j > 0, j < nj - 1))
        def _():
            def body(c, carry):
                r0 = pl.multiple_of(c * MOE_CHUNK, MOE_CHUNK)
                y_sc[pl.ds(r0, MOE_CHUNK), :] += _swiglu(xc_sc[pl.ds(r0, MOE_CHUNK), :], wg_ref, wu_ref, wd_ref)
                return carry

            lax.fori_loop(0, nch_sm[0], body, 0)

    @pl.when(j == nj - 1)
    def _():
        w_e = comb_ref[pl.ds(e, 1), :]

        def body(c, carry):
            r0 = pl.multiple_of(c * MOE_CHUNK, MOE_CHUNK)
            y = y_sc[pl.ds(r0, MOE_CHUNK), :] + _swiglu(xc_sc[pl.ds(r0, MOE_CHUNK), :], wg_ref, wu_ref, wd_ref)
            sel = jnp.where(one_hot(c), 1.0, 0.0)
            w_row = jnp.sum(sel * w_e, axis=1, keepdims=True)
            yw = (y * w_row).astype(BF16)
            o_ref[...] += jnp.dot(sel.T.astype(BF16), yw, preferred_element_type=F32)
            return carry

        lax.fori_loop(0, nch_sm[0], body, 0)

    @pl.when(jnp.logical_and(e == ne - 1, j == nj - 1))
    def _():
        o_ref[...] = _residual_out(x1_ref, g2_ref, o_ref[...], fg_ref)


def _moe(x1, h2, mod, comb_t, wg, wu, wd, final_g, *, tb, tf, seq_len):
    n, d = x1.shape
    ne, _, dff = wg.shape
    nj = dff // tf
    assert nj >= 2 and tb % MOE_CHUNK == 0 and n % tb == 0
    blocks_per_seq = max(seq_len // tb, 1)
    rows = mod.shape[1]
    final = final_g is not None
    pos = jnp.arange(tb)
    tri = (pos[:, None] < pos[None, :]).astype(BF16)
    tile = lambda i, e, j: (i, 0)
    in_specs = [pl.BlockSpec((tb, d), tile, pipeline_mode=pl.Buffered(1)),
                pl.BlockSpec((tb, d), tile),
                pl.BlockSpec((None, rows, d), lambda i, e, j: (i // blocks_per_seq, 0, 5)),
                pl.BlockSpec((ne, tb), lambda i, e, j: (0, i)),
                pl.BlockSpec((tb, tb), lambda i, e, j: (0, 0), pipeline_mode=pl.Buffered(1)),
                pl.BlockSpec((None, d, tf), lambda i, e, j: (e, 0, j)),
                pl.BlockSpec((None, d, tf), lambda i, e, j: (e, 0, j)),
                pl.BlockSpec((None, tf, d), lambda i, e, j: (e, j, 0))]
    args = [x1, h2, mod, comb_t, tri, wg, wu, wd]
    if final:
        in_specs.append(pl.BlockSpec((1, d), lambda i, e, j: (0, 0)))
        args.append(final_g)
    return pl.pallas_call(
        functools.partial(_moe_kernel, final=final, nj=nj),
        grid=(n // tb, ne, nj), in_specs=in_specs, out_specs=pl.BlockSpec((tb, d), tile),
        out_shape=jax.ShapeDtypeStruct((n, d), F32),
        scratch_shapes=[pltpu.VMEM((ne, tb), F32), pltpu.VMEM((tb, d), BF16), pltpu.VMEM((tb, d), F32),
                        pltpu.SMEM((1,), jnp.int32)],
        compiler_params=_cparams(("arbitrary", "arbitrary", "arbitrary")),
        name="moe",
    )(*args)


def _rope_tables(pos):
    half = ROT_DIM // 2
    inv = ROPE_THETA ** (-jnp.arange(half, dtype=F32) / half)
    ang = pos.astype(F32)[:, None] * inv[None, :]
    cos, sin = jnp.cos(ang), jnp.sin(ang)
    r = jnp.arange(LANES) % HEAD_DIM
    idx = r % half
    lo = (r < half)[None, :]
    hi = ((r >= half) & (r < ROT_DIM))[None, :]
    c = jnp.where((r < ROT_DIM)[None, :], cos[:, idx], 1.0)
    sa = jnp.where(lo, -sin[:, idx], 0.0)
    sb = jnp.where(hi, sin[:, idx], 0.0)
    return c, sa, sb


def _pick_ff_tile(dff, cap):
    best = None
    for t in range(LANES, cap + 1, LANES):
        if dff % t == 0:
            best = t
    return best if best is not None else dff


def kernel(x_prompt, x_sample, cache_k, cache_v, state_conv, page_table, c_prompt, c_sample, w_ada, b_ada, norm1_g, norm2_g, w_in, conv_w, lambda_q1, lambda_k1, lambda_q2, lambda_k2, subln_g, w_br_conv, w_br_attn, w_o, w_ff_gate, w_ff_up, w_ff_down, w_router, w_moe_gate, w_moe_up, w_moe_down, final_g):
    nseq, seq_len, d = x_prompt.shape
    nb, dec_seq, _ = x_sample.shape
    depth = w_ada.shape[0]
    page = cache_k.shape[2]
    past_len = page_table.shape[1] * page
    kvw = N_KV_HEADS * V_DIM
    n_p, n_s = nseq * seq_len, nb * dec_seq
    tm_p = min(512, seq_len)
    tq = min(1024, seq_len)
    tk = min(512, seq_len)
    tb_p = min(1024, seq_len)
    assert dec_seq == 8 and d % LANES == 0 and n_s % MOE_CHUNK == 0

    mod_all = _ada(jnp.concatenate([c_prompt, c_sample], axis=0), w_ada, b_ada)
    tabs_p = _rope_tables(jnp.arange(seq_len))
    tabs_s = tuple(jnp.tile(t, (nb, 1)) for t in _rope_tables(past_len + jnp.arange(dec_seq)))
    final_row = final_g.reshape(1, d)
    eye_kv = jnp.eye(N_KV_HEADS, dtype=BF16)

    xp = x_prompt.reshape(n_p, d)
    xs = x_sample.reshape(n_s, d)
    outs = [[] for _ in range(6)]
    for l in range(depth):
        lam_init = 0.8 - 0.6 * math.exp(-0.3 * l)
        mod_p = mod_all[l, :nseq].reshape(nseq, 1, 6 * d)
        mod_s = jnp.repeat(mod_all[l, nseq:], dec_seq, axis=0).reshape(1, n_s, 6 * d)
        w_in_l = w_in[l].astype(BF16)
        w_brc = w_br_conv[l].astype(BF16)
        w_bra = w_br_attn[l].astype(BF16)
        w_o_l = w_o[l].astype(BF16)
        ng1 = norm1_g[l].reshape(1, d)
        ng2 = norm2_g[l].reshape(1, d)
        lam_vecs = jnp.stack([lambda_q1[l], lambda_k1[l], lambda_q2[l], lambda_k2[l]])
        sg = subln_g[l].reshape(1, V_DIM)
        st = state_conv[l]
        state = (jnp.repeat(st[:, 0], dec_seq, axis=0), jnp.repeat(st[:, 1], dec_seq, axis=0))
        moe = l % 2 == 1
        i = l // 2
        if moe:
            wr = w_router[i].T
            wg, wu, wd = w_moe_gate[i].astype(BF16), w_moe_up[i].astype(BF16), w_moe_down[i].astype(BF16)
        else:
            wr = None
            wg, wu, wd = w_ff_gate[i].astype(BF16), w_ff_up[i].astype(BF16), w_ff_down[i].astype(BF16)
        tf = _pick_ff_tile(wg.shape[-1], 1792)
        fin = final_row if l == depth - 1 else None

        def mixer(res, mod, tm, tb, seq):
            if moe:
                return _moe(res[0], res[1], mod, res[2], wg, wu, wd, fin, tb=tb, tf=tf, seq_len=seq)
            return _ffn(res[0], res[1], mod, wg, wu, wd, fin, tm=tm, tf=tf, seq_len=seq)

        mconv, gate, q, kf, vf, kb, vt, cs = _inproj(
            xp, mod_p, ng1, w_in_l, w_brc, conv_w[l], tabs_p, None, tm=tm_p, seq_len=seq_len, sample=False, tk=tk)
        ya = _attn_prompt(q, kb, vt, lam_vecs, sg.reshape(V_DIM, 1), nseq=nseq, seq_len=seq_len, tq=tq,
                          lam_init=lam_init)
        res = _outproj(xp, mconv, gate, ya, mod_p, ng2, w_bra, w_o_l, wr, tm=tm_p, seq_len=seq_len)
        xp = mixer(res, mod_p, tm_p, tb_p, seq_len)
        outs[0].append(kf.reshape(nseq, seq_len, N_KV_HEADS, V_DIM))
        outs[1].append(vf.reshape(nseq, seq_len, N_KV_HEADS, V_DIM))
        outs[2].append(cs)

        mconv, gate, q, kf, vf, kb, vb, u = _inproj(
            xs, mod_s, ng1, w_in_l, w_brc, conv_w[l], tabs_s, state, tm=n_s, seq_len=n_s, sample=True)
        qe = q.reshape(N_KV_HEADS, GROUP, 2, nb, dec_seq, LANES).transpose(3, 0, 2, 1, 4, 5)
        qbd = (qe[:, :, :, :, :, None, :] * eye_kv[None, :, None, None, None, :, None]).reshape(
            nb, N_KV_HEADS * 2 * GROUP * dec_seq, kvw)
        pad = ((0, 0), (0, page - dec_seq), (0, 0))
        kn = jnp.pad(kb.reshape(nb, dec_seq, kvw), pad)
        vn = jnp.pad(vb.reshape(nb, dec_seq, kvw), pad)
        o = _attn_sample(page_table, qbd, kn, vn, lam_vecs, sg, cache_k, cache_v,
                         layer=l, pages=16, dec_seq=dec_seq, lam_init=lam_init)
        ya = o.reshape(nb, N_KV_HEADS, GROUP, dec_seq, V_DIM).transpose(0, 3, 1, 2, 4).reshape(n_s, d).astype(BF16)
        res = _outproj(xs, mconv, gate, ya, mod_s, ng2, w_bra, w_o_l, wr, tm=n_s, seq_len=n_s)
        xs = mixer(res, mod_s, n_s, n_s, n_s)
        outs[3].append(kf.reshape(nb, dec_seq, N_KV_HEADS, V_DIM))
        outs[4].append(vf.reshape(nb, dec_seq, N_KV_HEADS, V_DIM))
        outs[5].append(u.reshape(nb, dec_seq, d)[:, dec_seq - 2:])

    return (xp.reshape(nseq, seq_len, d), xs.reshape(nb, dec_seq, d),
            jnp.stack(outs[0]), jnp.stack(outs[1]), jnp.stack(outs[2]),
            jnp.stack(outs[3]), jnp.stack(outs[4]), jnp.stack(outs[5]))
```

```python
import functools
import math

import jax
import jax.numpy as jnp
from jax import lax
from jax.experimental import pallas as pl
from jax.experimental.pallas import tpu as pltpu

F32 = jnp.float32
BF16 = jnp.bfloat16

N_HEADS = 8
N_KV_HEADS = 4
GROUP = N_HEADS // N_KV_HEADS
HEAD_DIM = 64
V_DIM = 2 * HEAD_DIM
ROT_DIM = HEAD_DIM // 4
ROPE_THETA = 500000.0
TOP_K = 2
EPS = 1e-6

LANES = 128
V7X_VMEM_BYTES = 64 * 2 ** 20
VMEM_LIMIT = V7X_VMEM_BYTES - 8 * 2 ** 20

NEG_INF = float("-inf")
LOG2E = math.log2(math.e)


def _cparams(sem):
    return pltpu.CompilerParams(dimension_semantics=sem, vmem_limit_bytes=VMEM_LIMIT)


def _resident(shape):
    nd = len(shape)
    return pl.BlockSpec(shape, lambda *_: (0,) * nd, pipeline_mode=pl.Buffered(1))


def _mod_spec(rows, tiles_per_group, piece, d):
    return pl.BlockSpec((None, rows, d), lambda i: (i // tiles_per_group, 0, piece))


def _rmsnorm(x, g):
    return x * lax.rsqrt(jnp.mean(x * x, axis=-1, keepdims=True) + EPS) * g


def _ada_kernel(c_ref, w_ref, b_ref, o_ref):
    c = c_ref[...]
    a = (c * jax.nn.sigmoid(c)).astype(BF16)
    o_ref[...] = jnp.dot(a, w_ref[...].astype(BF16), preferred_element_type=F32) + b_ref[...]


def _ada(c_all, w_ada, b_ada):
    depth, d, d6 = w_ada.shape
    n = c_all.shape[0]
    return pl.pallas_call(
        _ada_kernel,
        grid=(depth, d6 // d),
        in_specs=[pl.BlockSpec((n, d), lambda l, j: (0, 0)),
                  pl.BlockSpec((None, d, d), lambda l, j: (l, 0, j)),
                  pl.BlockSpec((None, 1, d), lambda l, j: (l, 0, j))],
        out_specs=pl.BlockSpec((None, n, d), lambda l, j: (l, 0, j)),
        out_shape=jax.ShapeDtypeStruct((depth, n, d6), F32),
        compiler_params=_cparams(("arbitrary", "arbitrary")),
        name="ada",
    )(c_all, w_ada, b_ada.reshape(depth, 1, d6))


def _rope_chunk(xc, cos, sa, sb):
    half = ROT_DIM // 2
    return xc * cos + pltpu.roll(xc, LANES - half, 1) * sa + pltpu.roll(xc, half, 1) * sb


def _inproj_kernel(*refs, tm, d, sample, tiles_per_seq, n_carried):
    if sample:
        (x_ref, sh_ref, sc_ref, ng_ref, win_ref, wbr_ref, cw_ref, cos_ref, sa_ref, sb_ref, st0_ref, st1_ref,
         mconv_ref, gate_ref, q_ref, kf_ref, vf_ref, kb_ref, vb_ref, cs_ref, ubuf) = refs
    else:
        (x_ref, sh_ref, sc_ref, ng_ref, win_ref, wbr_ref, cw_ref, cos_ref, sa_ref, sb_ref) = refs[:10]
        (mconv_ref, gate_ref, q_ref, kf_ref, vf_ref, kb_ref, vt_ref, cs_ref, ubuf) = refs[10 + n_carried:]
    i = pl.program_id(0)
    kvw = N_KV_HEADS * V_DIM
    o_q = 3 * d
    o_k = o_q + N_HEADS * 2 * HEAD_DIM
    o_v = o_k + kvw
    o_ga = o_v + kvw
    o_gb = o_ga + d

    h = (_rmsnorm(x_ref[...], ng_ref[...]) * (1.0 + sc_ref[...]) + sh_ref[...]).astype(BF16)

    def proj(a, b):
        return jnp.dot(h, win_ref[:, a:b], preferred_element_type=F32)

    u = proj(d, 2 * d) * proj(2 * d, 3 * d)
    if sample:
        ubuf[0:8, :] = jnp.zeros((8, d), F32)
    else:
        @pl.when(i % tiles_per_seq == 0)
        def _():
            ubuf[0:8, :] = jnp.zeros((8, d), F32)
    ubuf[8:tm + 8, :] = u
    up1 = ubuf[7:tm + 7, :]
    up2 = ubuf[6:tm + 6, :]
    if sample:
        t = lax.broadcasted_iota(jnp.int32, (tm, d), 0) & 7
        st0 = st0_ref[...]
        st1 = st1_ref[...]
        up1 = jnp.where(t >= 1, up1, st1)
        up2 = jnp.where(t >= 2, up2, jnp.where(t == 1, st1, st0))
        cs_ref[...] = u
    else:
        cs_ref[...] = ubuf[tm + 6:tm + 8, :]
        ubuf[0:8, :] = ubuf[tm:tm + 8, :]
    cu = cw_ref[0:1, :] * up2 + cw_ref[1:2, :] * up1 + cw_ref[2:3, :] * u
    y_conv = (proj(0, d) * cu).astype(BF16)
    yc = jnp.dot(y_conv, wbr_ref[...], preferred_element_type=F32)
    mconv_ref[...] = (jax.nn.sigmoid(proj(o_ga, o_gb)) * yc).astype(BF16)
    gate_ref[...] = jax.nn.sigmoid(proj(o_gb, o_gb + d)).astype(BF16)

    cos = cos_ref[...]
    sa = sa_ref[...]
    sb = sb_ref[...]
    lane = lax.broadcasted_iota(jnp.int32, (tm, LANES), 1)
    zq = proj(o_q, o_k) * (HEAD_DIM ** -0.5 * LOG2E)
    for hq in range(N_HEADS):
        r = _rope_chunk(zq[:, hq * LANES:(hq + 1) * LANES], cos, sa, sb)
        kv, g = hq // GROUP, hq % GROUP
        q_ref[kv, 2 * g] = jnp.where(lane < HEAD_DIM, r, 0.0).astype(BF16)
        q_ref[kv, 2 * g + 1] = jnp.where(lane >= HEAD_DIM, r, 0.0).astype(BF16)
    zk = proj(o_k, o_v)
    zv = proj(o_v, o_ga)
    for kv in range(N_KV_HEADS):
        r = _rope_chunk(zk[:, kv * LANES:(kv + 1) * LANES], cos, sa, sb)
        kb_ref[:, kv * LANES:(kv + 1) * LANES] = r.astype(BF16)
        if sample:
            kf_ref[:, kv * LANES:(kv + 1) * LANES] = r
        else:
            kf_ref[pl.ds(kv, tm, stride=N_KV_HEADS), :] = r
            vf_ref[pl.ds(kv, tm, stride=N_KV_HEADS), :] = zv[:, kv * V_DIM:(kv + 1) * V_DIM]
    if sample:
        vf_ref[...] = zv
        vb_ref[...] = zv.astype(BF16)
    else:
        tk = vt_ref.shape[-1]
        for c in range(tm // tk):
            for kv in range(N_KV_HEADS):
                vt_ref[c, kv] = zv[c * tk:(c + 1) * tk, kv * V_DIM:(kv + 1) * V_DIM].T.astype(BF16)


def _inproj(x, mod, ng, w_in, w_br, cw, tabs, state, *, tm, seq_len, sample, tk=None, layer=0, depth=1,
            carried=None):
    n, d = x.shape
    nt = n // tm
    kvw = N_KV_HEADS * V_DIM
    tiles_per_seq = max(seq_len // tm, 1)
    table_tiles = tabs[0].shape[0] // tm
    rows = mod.shape[1]
    row = lambda i: (i, 0)
    in_specs = [pl.BlockSpec((tm, d), row),
                _mod_spec(rows, tiles_per_seq, 0, d), _mod_spec(rows, tiles_per_seq, 1, d),
                _resident((1, d)), _resident(w_in.shape), _resident(w_br.shape), _resident(cw.shape)]
    in_specs += [pl.BlockSpec((tm, LANES), lambda i: (i % table_tiles, 0))] * 3
    args = [x, mod, mod, ng, w_in, w_br, cw, *tabs]
    aliases = {}
    if sample:
        in_specs += [pl.BlockSpec((tm, d), row)] * 2
        args += list(state)
        cs_shape, cs_spec = (n, d), pl.BlockSpec((tm, d), row)
        v_shape, v_spec = (n, kvw), pl.BlockSpec((tm, kvw), row)
        kv_shape, kv_spec = (n, kvw), pl.BlockSpec((tm, kvw), row)
    else:
        nseq = n // seq_len
        cs_shape, cs_spec = (nseq, 2, d), pl.BlockSpec((None, 2, d), lambda i: (i // tiles_per_seq, 0, 0))
        v_shape = (n // tk, N_KV_HEADS, V_DIM, tk)
        v_spec = pl.BlockSpec((tm // tk, N_KV_HEADS, V_DIM, tk), lambda i: (i, 0, 0, 0))
        kv_shape = (depth, n * N_KV_HEADS, V_DIM)
        kv_spec = pl.BlockSpec((None, tm * N_KV_HEADS, V_DIM), lambda i: (layer, i, 0))
        if carried is not None:
            aliases = {len(args): 3, len(args) + 1: 4}
            in_specs += [pl.BlockSpec(memory_space=pl.ANY)] * 2
            args += list(carried)
    out_shape = [jax.ShapeDtypeStruct((n, d), BF16), jax.ShapeDtypeStruct((n, d), BF16),
                 jax.ShapeDtypeStruct((N_KV_HEADS, 2 * GROUP, n, LANES), BF16),
                 jax.ShapeDtypeStruct(kv_shape, F32), jax.ShapeDtypeStruct(kv_shape, F32),
                 jax.ShapeDtypeStruct((n, kvw), BF16), jax.ShapeDtypeStruct(v_shape, BF16),
                 jax.ShapeDtypeStruct(cs_shape, F32)]
    out_specs = [pl.BlockSpec((tm, d), row), pl.BlockSpec((tm, d), row),
                 pl.BlockSpec((N_KV_HEADS, 2 * GROUP, tm, LANES), lambda i: (0, 0, i, 0)),
                 kv_spec, kv_spec,
                 pl.BlockSpec((tm, kvw), row), v_spec, cs_spec]
    return pl.pallas_call(
        functools.partial(_inproj_kernel, tm=tm, d=d, sample=sample, tiles_per_seq=tiles_per_seq,
                          n_carried=len(aliases)),
        grid=(nt,), in_specs=in_specs, out_specs=out_specs, out_shape=out_shape,
        scratch_shapes=[pltpu.VMEM((tm + 8, d), F32)],
        input_output_aliases=aliases,
        compiler_params=_cparams(("arbitrary",)),
        name="inproj_sample" if sample else "inproj_prompt",
    )(*args)


def _lambda(lam_ref, lam_init):
    a = jnp.sum(lam_ref[0:1, :] * lam_ref[1:2, :], axis=-1, keepdims=True)
    b = jnp.sum(lam_ref[2:3, :] * lam_ref[3:4, :], axis=-1, keepdims=True)
    return jnp.exp(a) - jnp.exp(b) + lam_init


def _softmax_update(s, v, m_sc, l_sc, acc_sc):
    m_prev = m_sc[...]
    m_new = jnp.maximum(m_prev, jnp.max(s, axis=-1, keepdims=True))
    alpha = jnp.exp2(m_prev - m_new)
    p = jnp.exp2(s - m_new)
    l_sc[...] = alpha * l_sc[...] + jnp.sum(p, axis=-1, keepdims=True)
    acc_sc[...] = alpha * acc_sc[...] + jnp.dot(p.astype(BF16), v, preferred_element_type=F32)
    m_sc[...] = m_new


def _subln(o, g, lam_init):
    return _rmsnorm(o, g) * (1.0 - lam_init)


def _attn_prompt_kernel(q_ref, k_ref, vt_ref, lam_ref, g_ref, o_ref, m_sc, l_sc, acc_sc, *, tq, lam_init):
    qi = pl.program_id(2)
    nslab = 2 * GROUP
    tk = vt_ref.shape[-1]
    ratio = tq // tk
    m_sc[...] = jnp.full(m_sc.shape, NEG_INF, F32)
    l_sc[...] = jnp.zeros(l_sc.shape, F32)
    acc_sc[...] = jnp.zeros(acc_sc.shape, F32)

    def run_tiles(tiles):
        items = [(t, sl) for t in range(len(tiles)) for sl in range(nslab)]
        ks = [k_ref[pl.ds(pl.multiple_of(ki * tk, tk), tk), :] for ki, _ in tiles]

        def scores(n):
            t, sl = items[n]
            return lax.dot_general(ks[t], q_ref[sl], (((1,), (1,)), ((), ())), preferred_element_type=F32)

        st_next = scores(0)
        for n, (t, sl) in enumerate(items):
            ki, diag = tiles[t]
            st = st_next
            if n + 1 < len(items):
                st_next = scores(n + 1)
            if diag is not None:
                key = lax.broadcasted_iota(jnp.int32, (tk, tq), 0) + diag * tk
                tok = lax.broadcasted_iota(jnp.int32, (tk, tq), 1)
                st = jnp.where(key <= tok, st, NEG_INF)
            m_prev = m_sc[sl]
            m_new = jnp.maximum(m_prev, jnp.max(st, axis=0, keepdims=True))
            alpha = jnp.exp2(m_prev - m_new)
            p = jnp.exp2(st - m_new)
            l_sc[sl] = alpha * l_sc[sl] + jnp.sum(p, axis=0, keepdims=True)
            acc_sc[sl] = alpha * acc_sc[sl] + jnp.dot(vt_ref[ki], p.astype(BF16), preferred_element_type=F32)
            m_sc[sl] = m_new

    def body(kq, carry):
        run_tiles([(kq * ratio + r, None) for r in range(ratio)])
        return carry

    lax.fori_loop(0, qi, body, 0)
    run_tiles([(qi * ratio + dg, dg) for dg in range(ratio)])

    lam = _lambda(lam_ref, lam_init)
    def normalised(sl):
        return acc_sc[sl] * (1.0 / l_sc[sl])

    for g in range(GROUP):
        og = normalised(2 * g) - lam * normalised(2 * g + 1)
        ms = jnp.mean(og * og, axis=0, keepdims=True)
        y = og * lax.rsqrt(ms + EPS) * g_ref[...] * (1.0 - lam_init)
        o_ref[:, g * V_DIM:(g + 1) * V_DIM] = y.T.astype(BF16)


def _attn_prompt(q, kb, vt, lam_vecs, subln_col, *, nseq, seq_len, tq, lam_init):
    n = kb.shape[0]
    nq = seq_len // tq
    nslab = 2 * GROUP
    tk = vt.shape[-1]
    assert seq_len % tq == 0 and tq % tk == 0 and vt.shape == (n // tk, N_KV_HEADS, V_DIM, tk)
    return pl.pallas_call(
        functools.partial(_attn_prompt_kernel, tq=tq, lam_init=lam_init),
        grid=(nseq, N_KV_HEADS, nq),
        in_specs=[pl.BlockSpec((None, nslab, tq, LANES), lambda b, h, i: (h, 0, b * nq + i, 0)),
                  pl.BlockSpec((seq_len, V_DIM), lambda b, h, i: (b, h)),
                  pl.BlockSpec((seq_len // tk, None, V_DIM, tk), lambda b, h, i: (b, h, 0, 0)),
                  pl.BlockSpec(lam_vecs.shape, lambda b, h, i: (0, 0)),
                  pl.BlockSpec(subln_col.shape, lambda b, h, i: (0, 0))],
        out_specs=pl.BlockSpec((tq, GROUP * V_DIM), lambda b, h, i: (b * nq + i, h)),
        out_shape=jax.ShapeDtypeStruct((n, N_HEADS * V_DIM), BF16),
        scratch_shapes=[pltpu.VMEM((nslab, 1, tq), F32), pltpu.VMEM((nslab, 1, tq), F32),
                        pltpu.VMEM((nslab, V_DIM, tq), F32)],
        compiler_params=_cparams(("arbitrary", "arbitrary", "arbitrary")),
        name="attn_prompt",
    )(q, kb, vt, lam_vecs, subln_col)


def _attn_sample_kernel(pt_ref, q_ref, kn_ref, vn_ref, lam_ref, g_ref, *refs, pages, dec_seq, lam_init):
    k_refs = refs[:pages]
    v_refs = refs[pages:2 * pages]
    o_ref, m_sc, l_sc, acc_sc = refs[2 * pages:]
    step = pl.program_id(1)
    q = q_ref[...]
    nrow = q.shape[0]
    rows_per_head = nrow // N_KV_HEADS

    @pl.when(step == 0)
    def _():
        m_sc[...] = jnp.full(m_sc.shape, NEG_INF, F32)
        l_sc[...] = jnp.zeros(l_sc.shape, F32)
        acc_sc[...] = jnp.zeros(acc_sc.shape, F32)

    def page_rows(ref):
        page = ref.shape[0] // N_KV_HEADS
        return jnp.concatenate([ref[pl.ds(h, page, stride=N_KV_HEADS), :] for h in range(N_KV_HEADS)],
                               axis=1).astype(BF16)

    k = jnp.concatenate([page_rows(r) for r in k_refs], axis=0)
    v = jnp.concatenate([page_rows(r) for r in v_refs], axis=0)
    s = lax.dot_general(q, k, (((1,), (1,)), ((), ())), preferred_element_type=F32)
    _softmax_update(s, v, m_sc, l_sc, acc_sc)

    @pl.when(step == pl.num_programs(1) - 1)
    def _():
        kn = kn_ref[...]
        sn = lax.dot_general(q, kn, (((1,), (1,)), ((), ())), preferred_element_type=F32)
        t = lax.broadcasted_iota(jnp.int32, sn.shape, 0) % dec_seq
        col = lax.broadcasted_iota(jnp.int32, sn.shape, 1)
        sn = jnp.where(col <= t, sn, NEG_INF)
        _softmax_update(sn, vn_ref[...], m_sc, l_sc, acc_sc)

        lam = _lambda(lam_ref, lam_init)
        half = rows_per_head // 2
        for h in range(N_KV_HEADS):
            r0 = h * rows_per_head
            blk = acc_sc[r0:r0 + rows_per_head, h * V_DIM:(h + 1) * V_DIM] / l_sc[r0:r0 + rows_per_head, :]
            o = blk[0:half] - lam * blk[half:rows_per_head]
            o_ref[h] = _subln(o, g_ref[...], lam_init)


def _attn_sample(page_table, qbd, kn, vn, lam_vecs, subln_g, cache_k, cache_v, *, layer, pages, dec_seq, lam_init):
    nb, n_pages = page_table.shape
    page = cache_k.shape[2]
    kvw = N_KV_HEADS * V_DIM
    nrow = qbd.shape[1]
    assert n_pages % pages == 0
    ck = cache_k.reshape(cache_k.shape[0], cache_k.shape[1], page * N_KV_HEADS, V_DIM)
    cv = cache_v.reshape(cache_v.shape[0], cache_v.shape[1], page * N_KV_HEADS, V_DIM)

    def page_spec(j):
        return pl.BlockSpec((None, None, page * N_KV_HEADS, V_DIM),
                            lambda b, s, pt: (layer, pt[b * n_pages + s * pages + j], 0, 0))

    fixed = lambda shape: pl.BlockSpec(shape, lambda b, s, pt: (0,) * len(shape))
    per_b = lambda shape: pl.BlockSpec((None,) + shape, lambda b, s, pt: (b,) + (0,) * len(shape))
    grid_spec = pltpu.PrefetchScalarGridSpec(
        num_scalar_prefetch=1,
        grid=(nb, n_pages // pages),
        in_specs=[per_b((nrow, kvw)), per_b((page, kvw)), per_b((page, kvw)),
                  fixed(lam_vecs.shape), fixed(subln_g.shape)]
                 + [page_spec(j) for j in range(pages)] * 2,
        out_specs=per_b((N_KV_HEADS, nrow // N_KV_HEADS // 2, V_DIM)),
        scratch_shapes=[pltpu.VMEM((nrow, 1), F32), pltpu.VMEM((nrow, 1), F32), pltpu.VMEM((nrow, kvw), F32)],
    )
    return pl.pallas_call(
        functools.partial(_attn_sample_kernel, pages=pages, dec_seq=dec_seq, lam_init=lam_init),
        grid_spec=grid_spec,
        out_shape=jax.ShapeDtypeStruct((nb, N_KV_HEADS, nrow // N_KV_HEADS // 2, V_DIM), F32),
        compiler_params=_cparams(("arbitrary", "arbitrary")),
        name="attn_sample",
    )(page_table.reshape(-1), qbd, kn, vn, lam_vecs, subln_g, *([ck] * pages), *([cv] * pages))


def _outproj_kernel(*refs, moe):
    if moe:
        (x_ref, mconv_ref, gate_ref, ya_ref, g1_ref, sh2_ref, sc2_ref, ng_ref, wba_ref, wo_ref, wr_ref,
         x1_ref, h2_ref, comb_ref) = refs
    else:
        (x_ref, mconv_ref, gate_ref, ya_ref, g1_ref, sh2_ref, sc2_ref, ng_ref, wba_ref, wo_ref,
         x1_ref, h2_ref) = refs
    att = jnp.dot(ya_ref[...], wba_ref[...], preferred_element_type=F32)
    m = mconv_ref[...].astype(F32) + gate_ref[...].astype(F32) * att
    x1 = x_ref[...] + g1_ref[...] * jnp.dot(m.astype(BF16), wo_ref[...], preferred_element_type=F32)
    x1_ref[...] = x1
    h2 = _rmsnorm(x1, ng_ref[...]) * (1.0 + sc2_ref[...]) + sh2_ref[...]
    h2_ref[...] = h2.astype(BF16)
    if moe:
        logits = lax.dot_general(wr_ref[...], h2, (((1,), (1,)), ((), ())), preferred_element_type=F32,
                                 precision=lax.Precision.HIGHEST)
        ne = logits.shape[0]
        idx = lax.broadcasted_iota(jnp.int32, logits.shape, 0)
        v1 = jnp.max(logits, axis=0, keepdims=True)
        i1 = jnp.min(jnp.where(logits == v1, idx, ne), axis=0, keepdims=True)
        rest = jnp.where(idx == i1, NEG_INF, logits)
        v2 = jnp.max(rest, axis=0, keepdims=True)
        i2 = jnp.min(jnp.where(rest == v2, idx, ne), axis=0, keepdims=True)
        e2 = jnp.exp(v2 - v1)
        den = 1.0 + e2
        comb_ref[...] = jnp.where(idx == i1, 1.0 / den, 0.0) + jnp.where(idx == i2, e2 / den, 0.0)


def _outproj(x, mconv, gate, ya, mod, ng, w_ba, w_o, w_router, *, tm, seq_len):
    n, d = x.shape
    tiles_per_seq = max(seq_len // tm, 1)
    rows = mod.shape[1]
    moe = w_router is not None
    row = lambda i: (i, 0)
    tile = pl.BlockSpec((tm, d), row)
    in_specs = [tile, tile, tile, tile,
                _mod_spec(rows, tiles_per_seq, 2, d), _mod_spec(rows, tiles_per_seq, 3, d),
                _mod_spec(rows, tiles_per_seq, 4, d), _resident((1, d)), _resident(w_ba.shape), _resident(w_o.shape)]
    args = [x, mconv, gate, ya, mod, mod, mod, ng, w_ba, w_o]
    out_shape = [jax.ShapeDtypeStruct((n, d), F32), jax.ShapeDtypeStruct((n, d), BF16)]
    out_specs = [tile, tile]
    if moe:
        ne = w_router.shape[0]
        in_specs.append(_resident(w_router.shape))
        args.append(w_router)
        out_shape.append(jax.ShapeDtypeStruct((ne, n), F32))
        out_specs.append(pl.BlockSpec((ne, tm), lambda i: (0, i)))
    return pl.pallas_call(
        functools.partial(_outproj_kernel, moe=moe),
        grid=(n // tm,), in_specs=in_specs, out_specs=out_specs, out_shape=out_shape,
        compiler_params=_cparams(("arbitrary",)),
        name="outproj",
    )(*args)


def _swiglu(x, wg_ref, wu_ref, wd_ref):
    gate = jnp.dot(x, wg_ref[...], preferred_element_type=F32)
    up = jnp.dot(x, wu_ref[...], preferred_element_type=F32)
    act = (gate * jax.nn.sigmoid(gate) * up).astype(BF16)
    return jnp.dot(act, wd_ref[...], preferred_element_type=F32)


def _residual_out(x1_ref, g2_ref, f, fg_ref):
    x2 = x1_ref[...] + g2_ref[...] * f
    return x2 if fg_ref is None else _rmsnorm(x2, fg_ref[...])


def _ffn_kernel(*refs, final):
    if final:
        x1_ref, h2_ref, g2_ref, wg_ref, wu_ref, wd_ref, fg_ref, o_ref, acc_sc = refs
    else:
        x1_ref, h2_ref, g2_ref, wg_ref, wu_ref, wd_ref, o_ref, acc_sc = refs
        fg_ref = None
    j = pl.program_id(1)

    @pl.when(j == 0)
    def _():
        acc_sc[...] = jnp.zeros(acc_sc.shape, F32)

    acc_sc[...] += _swiglu(h2_ref[...], wg_ref, wu_ref, wd_ref)

    @pl.when(j == pl.num_programs(1) - 1)
    def _():
        o_ref[...] = _residual_out(x1_ref, g2_ref, acc_sc[...], fg_ref)


def _ffn(x1, h2, mod, wg, wu, wd, final_g, *, tm, tf, seq_len):
    n, d = x1.shape
    dff = wg.shape[1]
    tiles_per_seq = max(seq_len // tm, 1)
    rows = mod.shape[1]
    final = final_g is not None
    tile = pl.BlockSpec((tm, d), lambda i, j: (i, 0))
    in_specs = [tile, tile, pl.BlockSpec((None, rows, d), lambda i, j: (i // tiles_per_seq, 0, 5)),
                pl.BlockSpec((d, tf), lambda i, j: (0, j)), pl.BlockSpec((d, tf), lambda i, j: (0, j)),
                pl.BlockSpec((tf, d), lambda i, j: (j, 0))]
    args = [x1, h2, mod, wg, wu, wd]
    if final:
        in_specs.append(pl.BlockSpec((1, d), lambda i, j: (0, 0)))
        args.append(final_g)
    return pl.pallas_call(
        functools.partial(_ffn_kernel, final=final),
        grid=(n // tm, dff // tf), in_specs=in_specs, out_specs=tile,
        out_shape=jax.ShapeDtypeStruct((n, d), F32),
        scratch_shapes=[pltpu.VMEM((tm, d), F32)],
        compiler_params=_cparams(("arbitrary", "arbitrary")),
        name="ffn",
    )(*args)


MOE_CHUNK = 128


def _moe_kernel(*refs, final, nj):
    if final:
        (x1_ref, h2_ref, g2_ref, comb_ref, tri_ref, wg_ref, wu_ref, wd_ref, fg_ref,
         o_ref, rank_sc, xc_sc, y_sc, nch_sm) = refs
    else:
        (x1_ref, h2_ref, g2_ref, comb_ref, tri_ref, wg_ref, wu_ref, wd_ref,
         o_ref, rank_sc, xc_sc, y_sc, nch_sm) = refs
        fg_ref = None
    e = pl.program_id(1)
    j = pl.program_id(2)
    ne = pl.num_programs(1)
    tb = h2_ref.shape[0]

    @pl.when(jnp.logical_and(e == 0, j == 0))
    def _():
        routed = comb_ref[...] > 0.0
        cnt = jnp.dot(jnp.where(routed, 1.0, 0.0).astype(BF16), tri_ref[...], preferred_element_type=F32)
        rank_sc[...] = jnp.where(routed, cnt, -1.0)
        o_ref[...] = jnp.zeros(o_ref.shape, F32)

    rank = rank_sc[pl.ds(e, 1), :]

    def one_hot(c):
        slot = (lax.broadcasted_iota(jnp.int32, (MOE_CHUNK, tb), 0) + c * MOE_CHUNK).astype(F32)
        return rank == slot

    @pl.when(j == 0)
    def _():
        n_tok = jnp.sum(jnp.where(rank >= 0.0, 1.0, 0.0)).astype(jnp.int32)
        nch = (n_tok + (MOE_CHUNK - 1)) // MOE_CHUNK
        nch_sm[0] = nch

        def body(c, carry):
            r0 = pl.multiple_of(c * MOE_CHUNK, MOE_CHUNK)
            sel = jnp.where(one_hot(c), 1.0, 0.0).astype(BF16)
            xc = jnp.dot(sel, h2_ref[...], preferred_element_type=F32).astype(BF16)
            xc_sc[pl.ds(r0, MOE_CHUNK), :] = xc
            y_sc[pl.ds(r0, MOE_CHUNK), :] = _swiglu(xc, wg_ref, wu_ref, wd_ref)
            return carry

        lax.fori_loop(0, nch, body, 0)

    if nj > 2:
        @pl.when(jnp.logical_and(j > 0, j < nj - 1))
        def _():
            def body(c, carry):
                r0 = pl.multiple_of(c * MOE_CHUNK, MOE_CHUNK)
                y_sc[pl.ds(r0, MOE_CHUNK), :] += _swiglu(xc_sc[pl.ds(r0, MOE_CHUNK), :], wg_ref, wu_ref, wd_ref)
                return carry

            lax.fori_loop(0, nch_sm[0], body, 0)

    @pl.when(j == nj - 1)
    def _():
        w_e = comb_ref[pl.ds(e, 1), :]

        def body(c, carry):
            r0 = pl.multiple_of(c * MOE_CHUNK, MOE_CHUNK)
            y = y_sc[pl.ds(r0, MOE_CHUNK), :] + _swiglu(xc_sc[pl.ds(r0, MOE_CHUNK), :], wg_ref, wu_ref, wd_ref)
            sel = jnp.where(one_hot(c), 1.0, 0.0)
            w_row = jnp.sum(sel * w_e, axis=1, keepdims=True)
            yw = (y * w_row).astype(BF16)
            o_ref[...] += jnp.dot(sel.T.astype(BF16), yw, preferred_element_type=F32)
            return carry

        lax.fori_loop(0, nch_sm[0], body, 0)

    @pl.when(jnp.logical_and(e == ne - 1, j == nj - 1))
    def _():
        o_ref[...] = _residual_out(x1_ref, g2_ref, o_ref[...], fg_ref)


def _moe(x1, h2, mod, comb_t, wg, wu, wd, final_g, *, tb, tf, seq_len):
    n, d = x1.shape
    ne, _, dff = wg.shape
    nj = dff // tf
    assert nj >= 2 and tb % MOE_CHUNK == 0 and n % tb == 0
    blocks_per_seq = max(seq_len // tb, 1)
    rows = mod.shape[1]
    final = final_g is not None
    pos = jnp.arange(tb)
    tri = (pos[:, None] < pos[None, :]).astype(BF16)
    tile = lambda i, e, j: (i, 0)
    in_specs = [pl.BlockSpec((tb, d), tile, pipeline_mode=pl.Buffered(1)),
                pl.BlockSpec((tb, d), tile),
                pl.BlockSpec((None, rows, d), lambda i, e, j: (i // blocks_per_seq, 0, 5)),
                pl.BlockSpec((ne, tb), lambda i, e, j: (0, i)),
                pl.BlockSpec((tb, tb), lambda i, e, j: (0, 0), pipeline_mode=pl.Buffered(1)),
                pl.BlockSpec((None, d, tf), lambda i, e, j: (e, 0, j)),
                pl.BlockSpec((None, d, tf), lambda i, e, j: (e, 0, j)),
                pl.BlockSpec((None, tf, d), lambda i, e, j: (e, j, 0))]
    args = [x1, h2, mod, comb_t, tri, wg, wu, wd]
    if final:
        in_specs.append(pl.BlockSpec((1, d), lambda i, e, j: (0, 0)))
        args.append(final_g)
    return pl.pallas_call(
        functools.partial(_moe_kernel, final=final, nj=nj),
        grid=(n // tb, ne, nj), in_specs=in_specs, out_specs=pl.BlockSpec((tb, d), tile),
        out_shape=jax.ShapeDtypeStruct((n, d), F32),
        scratch_shapes=[pltpu.VMEM((ne, tb), F32), pltpu.VMEM((tb, d), BF16), pltpu.VMEM((tb, d), F32),
                        pltpu.SMEM((1,), jnp.int32)],
        compiler_params=_cparams(("arbitrary", "arbitrary", "arbitrary")),
        name="moe",
    )(*args)


def _rope_tables(pos):
    half = ROT_DIM // 2
    inv = ROPE_THETA ** (-jnp.arange(half, dtype=F32) / half)
    ang = pos.astype(F32)[:, None] * inv[None, :]
    cos, sin = jnp.cos(ang), jnp.sin(ang)
    r = jnp.arange(LANES) % HEAD_DIM
    idx = r % half
    lo = (r < half)[None, :]
    hi = ((r >= half) & (r < ROT_DIM))[None, :]
    c = jnp.where((r < ROT_DIM)[None, :], cos[:, idx], 1.0)
    sa = jnp.where(lo, -sin[:, idx], 0.0)
    sb = jnp.where(hi, sin[:, idx], 0.0)
    return c, sa, sb


def _pick_ff_tile(dff, cap):
    best = None
    for t in range(LANES, cap + 1, LANES):
        if dff % t == 0:
            best = t
    return best if best is not None else dff


def kernel(x_prompt, x_sample, cache_k, cache_v, state_conv, page_table, c_prompt, c_sample, w_ada, b_ada, norm1_g, norm2_g, w_in, conv_w, lambda_q1, lambda_k1, lambda_q2, lambda_k2, subln_g, w_br_conv, w_br_attn, w_o, w_ff_gate, w_ff_up, w_ff_down, w_router, w_moe_gate, w_moe_up, w_moe_down, final_g):
    nseq, seq_len, d = x_prompt.shape
    nb, dec_seq, _ = x_sample.shape
    depth = w_ada.shape[0]
    page = cache_k.shape[2]
    past_len = page_table.shape[1] * page
    kvw = N_KV_HEADS * V_DIM
    n_p, n_s = nseq * seq_len, nb * dec_seq
    tm_p = min(512, seq_len)
    tq = min(1024, seq_len)
    tk = min(512, seq_len)
    tb_p = min(1024, seq_len)
    assert dec_seq == 8 and d % LANES == 0 and n_s % MOE_CHUNK == 0

    mod_all = _ada(jnp.concatenate([c_prompt, c_sample], axis=0), w_ada, b_ada)
    tabs_p = _rope_tables(jnp.arange(seq_len))
    tabs_s = tuple(jnp.tile(t, (nb, 1)) for t in _rope_tables(past_len + jnp.arange(dec_seq)))
    final_row = final_g.reshape(1, d)
    eye_kv = jnp.eye(N_KV_HEADS, dtype=BF16)

    xp = x_prompt.reshape(n_p, d)
    xs = x_sample.reshape(n_s, d)
    outs = [[] for _ in range(6)]
    new_kv_p = None
    for l in range(depth):
        lam_init = 0.8 - 0.6 * math.exp(-0.3 * l)
        mod_p = mod_all[l, :nseq].reshape(nseq, 1, 6 * d)
        mod_s = jnp.repeat(mod_all[l, nseq:], dec_seq, axis=0).reshape(1, n_s, 6 * d)
        w_in_l = w_in[l].astype(BF16)
        w_brc = w_br_conv[l].astype(BF16)
        w_bra = w_br_attn[l].astype(BF16)
        w_o_l = w_o[l].astype(BF16)
        ng1 = norm1_g[l].reshape(1, d)
        ng2 = norm2_g[l].reshape(1, d)
        lam_vecs = jnp.stack([lambda_q1[l], lambda_k1[l], lambda_q2[l], lambda_k2[l]])
        sg = subln_g[l].reshape(1, V_DIM)
        st = state_conv[l]
        state = (jnp.repeat(st[:, 0], dec_seq, axis=0), jnp.repeat(st[:, 1], dec_seq, axis=0))
        moe = l % 2 == 1
        i = l // 2
        if moe:
            wr = w_router[i].T
            wg, wu, wd = w_moe_gate[i].astype(BF16), w_moe_up[i].astype(BF16), w_moe_down[i].astype(BF16)
        else:
            wr = None
            wg, wu, wd = w_ff_gate[i].astype(BF16), w_ff_up[i].astype(BF16), w_ff_down[i].astype(BF16)
        tf = _pick_ff_tile(wg.shape[-1], 1792)
        fin = final_row if l == depth - 1 else None

        def mixer(res, mod, tm, tb, seq):
            if moe:
                return _moe(res[0], res[1], mod, res[2], wg, wu, wd, fin, tb=tb, tf=tf, seq_len=seq)
            return _ffn(res[0], res[1], mod, wg, wu, wd, fin, tm=tm, tf=tf, seq_len=seq)

        mconv, gate, q, kf, vf, kb, vt, cs = _inproj(
            xp, mod_p, ng1, w_in_l, w_brc, conv_w[l], tabs_p, None, tm=tm_p, seq_len=seq_len, sample=False, tk=tk,
            layer=l, depth=depth, carried=new_kv_p)
        new_kv_p = (kf, vf)
        ya = _attn_prompt(q, kb, vt, lam_vecs, sg.reshape(V_DIM, 1), nseq=nseq, seq_len=seq_len, tq=tq,
                          lam_init=lam_init)
        res = _outproj(xp, mconv, gate, ya, mod_p, ng2, w_bra, w_o_l, wr, tm=tm_p, seq_len=seq_len)
        xp = mixer(res, mod_p, tm_p, tb_p, seq_len)
        outs[2].append(cs)

        mconv, gate, q, kf, vf, kb, vb, u = _inproj(
            xs, mod_s, ng1, w_in_l, w_brc, conv_w[l], tabs_s, state, tm=n_s, seq_len=n_s, sample=True)
        qe = q.reshape(N_KV_HEADS, GROUP, 2, nb, dec_seq, LANES).transpose(3, 0, 2, 1, 4, 5)
        qbd = (qe[:, :, :, :, :, None, :] * eye_kv[None, :, None, None, None, :, None]).reshape(
            nb, N_KV_HEADS * 2 * GROUP * dec_seq, kvw)
        pad = ((0, 0), (0, page - dec_seq), (0, 0))
        kn = jnp.pad(kb.reshape(nb, dec_seq, kvw), pad)
        vn = jnp.pad(vb.reshape(nb, dec_seq, kvw), pad)
        o = _attn_sample(page_table, qbd, kn, vn, lam_vecs, sg, cache_k, cache_v,
                         layer=l, pages=min(32, page_table.shape[1]), dec_seq=dec_seq, lam_init=lam_init)
        ya = o.reshape(nb, N_KV_HEADS, GROUP, dec_seq, V_DIM).transpose(0, 3, 1, 2, 4).reshape(n_s, d).astype(BF16)
        res = _outproj(xs, mconv, gate, ya, mod_s, ng2, w_bra, w_o_l, wr, tm=n_s, seq_len=n_s)
        xs = mixer(res, mod_s, n_s, n_s, n_s)
        outs[3].append(kf.reshape(nb, dec_seq, N_KV_HEADS, V_DIM))
        outs[4].append(vf.reshape(nb, dec_seq, N_KV_HEADS, V_DIM))
        outs[5].append(u.reshape(nb, dec_seq, d)[:, dec_seq - 2:])

    kv_shape = (depth, nseq, seq_len, N_KV_HEADS, V_DIM)
    return (xp.reshape(nseq, seq_len, d), xs.reshape(nb, dec_seq, d),
            new_kv_p[0].reshape(kv_shape), new_kv_p[1].reshape(kv_shape), jnp.stack(outs[2]),
            jnp.stack(outs[3]), jnp.stack(outs[4]), jnp.stack(outs[5]))
```

```python
import functools
import math

import jax
import jax.numpy as jnp
from jax import lax
from jax.experimental import pallas as pl
from jax.experimental.pallas import tpu as pltpu

F32 = jnp.float32
BF16 = jnp.bfloat16

N_HEADS = 8
N_KV_HEADS = 4
GROUP = N_HEADS // N_KV_HEADS
HEAD_DIM = 64
V_DIM = 2 * HEAD_DIM
ROT_DIM = HEAD_DIM // 4
ROPE_THETA = 500000.0
TOP_K = 2
EPS = 1e-6

LANES = 128
V7X_VMEM_BYTES = 64 * 2 ** 20
VMEM_LIMIT = V7X_VMEM_BYTES - 8 * 2 ** 20

NEG_INF = float("-inf")
LOG2E = math.log2(math.e)


def _cparams(sem):
    return pltpu.CompilerParams(dimension_semantics=sem, vmem_limit_bytes=VMEM_LIMIT)


def _resident(shape):
    nd = len(shape)
    return pl.BlockSpec(shape, lambda *_: (0,) * nd, pipeline_mode=pl.Buffered(1))


def _mod_spec(rows, tiles_per_group, piece, d):
    return pl.BlockSpec((None, rows, d), lambda i: (i // tiles_per_group, 0, piece))


def _rmsnorm(x, g):
    return x * lax.rsqrt(jnp.mean(x * x, axis=-1, keepdims=True) + EPS) * g


def _ada_kernel(c_ref, w_ref, b_ref, o_ref):
    c = c_ref[...]
    a = (c * jax.nn.sigmoid(c)).astype(BF16)
    o_ref[...] = jnp.dot(a, w_ref[...].astype(BF16), preferred_element_type=F32) + b_ref[...]


def _ada(c_all, w_ada, b_ada):
    depth, d, d6 = w_ada.shape
    n = c_all.shape[0]
    return pl.pallas_call(
        _ada_kernel,
        grid=(depth, d6 // d),
        in_specs=[pl.BlockSpec((n, d), lambda l, j: (0, 0)),
                  pl.BlockSpec((None, d, d), lambda l, j: (l, 0, j)),
                  pl.BlockSpec((None, 1, d), lambda l, j: (l, 0, j))],
        out_specs=pl.BlockSpec((None, n, d), lambda l, j: (l, 0, j)),
        out_shape=jax.ShapeDtypeStruct((depth, n, d6), F32),
        compiler_params=_cparams(("arbitrary", "arbitrary")),
        name="ada",
    )(c_all, w_ada, b_ada.reshape(depth, 1, d6))


def _rope_chunk(xc, cos, sa, sb):
    half = ROT_DIM // 2
    return xc * cos + pltpu.roll(xc, LANES - half, 1) * sa + pltpu.roll(xc, half, 1) * sb


def _inproj_kernel(*refs, tm, d, sample, tiles_per_seq, n_carried):
    if sample:
        (x_ref, sh_ref, sc_ref, ng_ref, win_ref, wbr_ref, cw_ref, cos_ref, sa_ref, sb_ref, st0_ref, st1_ref,
         mconv_ref, gate_ref, q_ref, kf_ref, vf_ref, kb_ref, vb_ref, cs_ref, ubuf) = refs
    else:
        (x_ref, sh_ref, sc_ref, ng_ref, win_ref, wbr_ref, cw_ref, cos_ref, sa_ref, sb_ref) = refs[:10]
        (mconv_ref, gate_ref, q_ref, kf_ref, vf_ref, kb_ref, vt_ref, cs_ref, ubuf) = refs[10 + n_carried:]
    i = pl.program_id(0)
    kvw = N_KV_HEADS * V_DIM
    o_q = 3 * d
    o_k = o_q + N_HEADS * 2 * HEAD_DIM
    o_v = o_k + kvw
    o_ga = o_v + kvw
    o_gb = o_ga + d

    h = (_rmsnorm(x_ref[...], ng_ref[...]) * (1.0 + sc_ref[...]) + sh_ref[...]).astype(BF16)

    def proj(a, b):
        return jnp.dot(h, win_ref[:, a:b], preferred_element_type=F32)

    u = proj(d, 2 * d) * proj(2 * d, 3 * d)
    if sample:
        ubuf[0:8, :] = jnp.zeros((8, d), F32)
    else:
        @pl.when(i % tiles_per_seq == 0)
        def _():
            ubuf[0:8, :] = jnp.zeros((8, d), F32)
    ubuf[8:tm + 8, :] = u
    up1 = ubuf[7:tm + 7, :]
    up2 = ubuf[6:tm + 6, :]
    if sample:
        t = lax.broadcasted_iota(jnp.int32, (tm, d), 0) & 7
        st0 = st0_ref[...]
        st1 = st1_ref[...]
        up1 = jnp.where(t >= 1, up1, st1)
        up2 = jnp.where(t >= 2, up2, jnp.where(t == 1, st1, st0))
        cs_ref[...] = u
    else:
        cs_ref[...] = ubuf[tm + 6:tm + 8, :]
        ubuf[0:8, :] = ubuf[tm:tm + 8, :]
    cu = cw_ref[0:1, :] * up2 + cw_ref[1:2, :] * up1 + cw_ref[2:3, :] * u
    y_conv = (proj(0, d) * cu).astype(BF16)
    yc = jnp.dot(y_conv, wbr_ref[...], preferred_element_type=F32)
    mconv_ref[...] = (jax.nn.sigmoid(proj(o_ga, o_gb)) * yc).astype(BF16)
    gate_ref[...] = jax.nn.sigmoid(proj(o_gb, o_gb + d)).astype(BF16)

    cos = cos_ref[...]
    sa = sa_ref[...]
    sb = sb_ref[...]
    lane = lax.broadcasted_iota(jnp.int32, (tm, LANES), 1)
    zq = proj(o_q, o_k) * (HEAD_DIM ** -0.5 * LOG2E)
    for hq in range(N_HEADS):
        r = _rope_chunk(zq[:, hq * LANES:(hq + 1) * LANES], cos, sa, sb)
        kv, g = hq // GROUP, hq % GROUP
        q_ref[kv, 2 * g] = jnp.where(lane < HEAD_DIM, r, 0.0).astype(BF16)
        q_ref[kv, 2 * g + 1] = jnp.where(lane >= HEAD_DIM, r, 0.0).astype(BF16)
    zk = proj(o_k, o_v)
    zv = proj(o_v, o_ga)
    for kv in range(N_KV_HEADS):
        r = _rope_chunk(zk[:, kv * LANES:(kv + 1) * LANES], cos, sa, sb)
        kb_ref[:, kv * LANES:(kv + 1) * LANES] = r.astype(BF16)
        if sample:
            kf_ref[:, kv * LANES:(kv + 1) * LANES] = r
        else:
            kf_ref[pl.ds(kv, tm, stride=N_KV_HEADS), :] = r
            vf_ref[pl.ds(kv, tm, stride=N_KV_HEADS), :] = zv[:, kv * V_DIM:(kv + 1) * V_DIM]
    if sample:
        vf_ref[...] = zv
        vb_ref[...] = zv.astype(BF16)
    else:
        tk = vt_ref.shape[-1]
        for c in range(tm // tk):
            for kv in range(N_KV_HEADS):
                vt_ref[c, kv] = zv[c * tk:(c + 1) * tk, kv * V_DIM:(kv + 1) * V_DIM].T.astype(BF16)


def _inproj(x, mod, ng, w_in, w_br, cw, tabs, state, *, tm, seq_len, sample, tk=None, layer=0, depth=1,
            carried=None):
    n, d = x.shape
    nt = n // tm
    kvw = N_KV_HEADS * V_DIM
    tiles_per_seq = max(seq_len // tm, 1)
    table_tiles = tabs[0].shape[0] // tm
    rows = mod.shape[1]
    row = lambda i: (i, 0)
    in_specs = [pl.BlockSpec((tm, d), row),
                _mod_spec(rows, tiles_per_seq, 0, d), _mod_spec(rows, tiles_per_seq, 1, d),
                _resident((1, d)), _resident(w_in.shape), _resident(w_br.shape), _resident(cw.shape)]
    in_specs += [pl.BlockSpec((tm, LANES), lambda i: (i % table_tiles, 0))] * 3
    args = [x, mod, mod, ng, w_in, w_br, cw, *tabs]
    aliases = {}
    if sample:
        in_specs += [pl.BlockSpec((tm, d), row)] * 2
        args += list(state)
        cs_shape, cs_spec = (n, d), pl.BlockSpec((tm, d), row)
        v_shape, v_spec = (n, kvw), pl.BlockSpec((tm, kvw), row)
        kv_shape, kv_spec = (n, kvw), pl.BlockSpec((tm, kvw), row)
    else:
        nseq = n // seq_len
        cs_shape, cs_spec = (nseq, 2, d), pl.BlockSpec((None, 2, d), lambda i: (i // tiles_per_seq, 0, 0))
        v_shape = (n // tk, N_KV_HEADS, V_DIM, tk)
        v_spec = pl.BlockSpec((tm // tk, N_KV_HEADS, V_DIM, tk), lambda i: (i, 0, 0, 0))
        kv_shape = (depth, n * N_KV_HEADS, V_DIM)
        kv_spec = pl.BlockSpec((None, tm * N_KV_HEADS, V_DIM), lambda i: (layer, i, 0))
        if carried is not None:
            aliases = {len(args): 3, len(args) + 1: 4}
            in_specs += [pl.BlockSpec(memory_space=pl.ANY)] * 2
            args += list(carried)
    out_shape = [jax.ShapeDtypeStruct((n, d), BF16), jax.ShapeDtypeStruct((n, d), BF16),
                 jax.ShapeDtypeStruct((N_KV_HEADS, 2 * GROUP, n, LANES), BF16),
                 jax.ShapeDtypeStruct(kv_shape, F32), jax.ShapeDtypeStruct(kv_shape, F32),
                 jax.ShapeDtypeStruct((n, kvw), BF16), jax.ShapeDtypeStruct(v_shape, BF16),
                 jax.ShapeDtypeStruct(cs_shape, F32)]
    out_specs = [pl.BlockSpec((tm, d), row), pl.BlockSpec((tm, d), row),
                 pl.BlockSpec((N_KV_HEADS, 2 * GROUP, tm, LANES), lambda i: (0, 0, i, 0)),
                 kv_spec, kv_spec,
                 pl.BlockSpec((tm, kvw), row), v_spec, cs_spec]
    return pl.pallas_call(
        functools.partial(_inproj_kernel, tm=tm, d=d, sample=sample, tiles_per_seq=tiles_per_seq,
                          n_carried=len(aliases)),
        grid=(nt,), in_specs=in_specs, out_specs=out_specs, out_shape=out_shape,
        scratch_shapes=[pltpu.VMEM((tm + 8, d), F32)],
        input_output_aliases=aliases,
        compiler_params=_cparams(("arbitrary",)),
        name="inproj_sample" if sample else "inproj_prompt",
    )(*args)


def _lambda(lam_ref, lam_init):
    a = jnp.sum(lam_ref[0:1, :] * lam_ref[1:2, :], axis=-1, keepdims=True)
    b = jnp.sum(lam_ref[2:3, :] * lam_ref[3:4, :], axis=-1, keepdims=True)
    return jnp.exp(a) - jnp.exp(b) + lam_init


def _softmax_update(s, v, m_sc, l_sc, acc_sc):
    m_prev = m_sc[...]
    m_new = jnp.maximum(m_prev, jnp.max(s, axis=-1, keepdims=True))
    alpha = jnp.exp2(m_prev - m_new)
    p = jnp.exp2(s - m_new)
    l_sc[...] = alpha * l_sc[...] + jnp.sum(p, axis=-1, keepdims=True)
    acc_sc[...] = alpha * acc_sc[...] + jnp.dot(p.astype(BF16), v, preferred_element_type=F32)
    m_sc[...] = m_new


def _subln(o, g, lam_init):
    return _rmsnorm(o, g) * (1.0 - lam_init)


AHEAD = 2
ONES_ROWS = 16


def _attn_prompt_kernel(q_ref, k_ref, vt_ref, lam_ref, g_ref, o_ref, m_sc, acc_sc, *, tq, lam_init):
    qi = pl.program_id(2)
    nslab = 2 * GROUP
    tk = vt_ref.shape[-1]
    ratio = tq // tk
    strip = tk
    m_sc[...] = jnp.full(m_sc.shape, NEG_INF, F32)
    acc_sc[...] = jnp.zeros(acc_sc.shape, F32)

    def run_tiles(tiles):
        items = [(t, sl, c) for t, (_, diag) in enumerate(tiles) for sl in range(nslab)
                 for c in range(tq // strip) if diag is None or diag * tk < (c + 1) * strip]
        ks = [k_ref[pl.ds(pl.multiple_of(ki * tk, tk), tk), :] for ki, _ in tiles]

        def scores(n):
            t, sl, c = items[n]
            return lax.dot_general(ks[t], q_ref[sl, c * strip:(c + 1) * strip, :], (((1,), (1,)), ((), ())),
                                   preferred_element_type=F32)

        pending = [scores(n) for n in range(min(AHEAD, len(items)))]
        for n, (t, sl, c) in enumerate(items):
            ki, diag = tiles[t]
            cols = slice(c * strip, (c + 1) * strip)
            st = pending.pop(0)
            if n + AHEAD < len(items):
                pending.append(scores(n + AHEAD))
            if diag is not None and (diag + 1) * tk > c * strip + 1:
                key = lax.broadcasted_iota(jnp.int32, (tk, strip), 0) + diag * tk
                tok = lax.broadcasted_iota(jnp.int32, (tk, strip), 1) + c * strip
                st = jnp.where(key <= tok, st, NEG_INF)
            m_prev = m_sc[sl, :, cols]
            m_new = jnp.maximum(m_prev, jnp.max(st, axis=0, keepdims=True))
            alpha = jnp.exp2(m_prev - m_new)
            p = jnp.exp2(st - m_new).astype(BF16)
            vt1 = jnp.concatenate([vt_ref[ki], jnp.ones((ONES_ROWS, tk), BF16)], axis=0)
            acc_sc[sl, :, cols] = alpha * acc_sc[sl, :, cols] + jnp.dot(vt1, p, preferred_element_type=F32)
            m_sc[sl, :, cols] = m_new

    def body(kq, carry):
        run_tiles([(kq * ratio + r, None) for r in range(ratio)])
        return carry

    lax.fori_loop(0, qi, body, 0)
    run_tiles([(qi * ratio + dg, dg) for dg in range(ratio)])

    lam = _lambda(lam_ref, lam_init)
    def normalised(sl):
        return acc_sc[sl, 0:V_DIM, :] * (1.0 / acc_sc[sl, V_DIM:V_DIM + 1, :])

    for g in range(GROUP):
        og = normalised(2 * g) - lam * normalised(2 * g + 1)
        ms = jnp.mean(og * og, axis=0, keepdims=True)
        y = og * lax.rsqrt(ms + EPS) * g_ref[...] * (1.0 - lam_init)
        o_ref[:, g * V_DIM:(g + 1) * V_DIM] = y.T.astype(BF16)


def _attn_prompt(q, kb, vt, lam_vecs, subln_col, *, nseq, seq_len, tq, lam_init):
    n = kb.shape[0]
    nq = seq_len // tq
    nslab = 2 * GROUP
    tk = vt.shape[-1]
    assert seq_len % tq == 0 and tq % tk == 0 and vt.shape == (n // tk, N_KV_HEADS, V_DIM, tk)
    return pl.pallas_call(
        functools.partial(_attn_prompt_kernel, tq=tq, lam_init=lam_init),
        grid=(nseq, N_KV_HEADS, nq),
        in_specs=[pl.BlockSpec((None, nslab, tq, LANES), lambda b, h, i: (h, 0, b * nq + i, 0)),
                  pl.BlockSpec((seq_len, V_DIM), lambda b, h, i: (b, h)),
                  pl.BlockSpec((seq_len // tk, None, V_DIM, tk), lambda b, h, i: (b, h, 0, 0)),
                  pl.BlockSpec(lam_vecs.shape, lambda b, h, i: (0, 0)),
                  pl.BlockSpec(subln_col.shape, lambda b, h, i: (0, 0))],
        out_specs=pl.BlockSpec((tq, GROUP * V_DIM), lambda b, h, i: (b * nq + i, h)),
        out_shape=jax.ShapeDtypeStruct((n, N_HEADS * V_DIM), BF16),
        scratch_shapes=[pltpu.VMEM((nslab, 1, tq), F32), pltpu.VMEM((nslab, V_DIM + ONES_ROWS, tq), F32)],
        compiler_params=_cparams(("arbitrary", "arbitrary", "arbitrary")),
        name="attn_prompt",
    )(q, kb, vt, lam_vecs, subln_col)


def _attn_sample_kernel(pt_ref, q_ref, kn_ref, vn_ref, lam_ref, g_ref, *refs, pages, dec_seq, lam_init):
    k_refs = refs[:pages]
    v_refs = refs[pages:2 * pages]
    o_ref, m_sc, l_sc, acc_sc = refs[2 * pages:]
    step = pl.program_id(1)
    q = q_ref[...]
    nrow = q.shape[0]
    rows_per_head = nrow // N_KV_HEADS

    @pl.when(step == 0)
    def _():
        m_sc[...] = jnp.full(m_sc.shape, NEG_INF, F32)
        l_sc[...] = jnp.zeros(l_sc.shape, F32)
        acc_sc[...] = jnp.zeros(acc_sc.shape, F32)

    def page_rows(ref):
        page = ref.shape[0] // N_KV_HEADS
        return jnp.concatenate([ref[pl.ds(h, page, stride=N_KV_HEADS), :] for h in range(N_KV_HEADS)],
                               axis=1).astype(BF16)

    k = jnp.concatenate([page_rows(r) for r in k_refs], axis=0)
    v = jnp.concatenate([page_rows(r) for r in v_refs], axis=0)
    s = lax.dot_general(q, k, (((1,), (1,)), ((), ())), preferred_element_type=F32)
    _softmax_update(s, v, m_sc, l_sc, acc_sc)

    @pl.when(step == pl.num_programs(1) - 1)
    def _():
        kn = kn_ref[...]
        sn = lax.dot_general(q, kn, (((1,), (1,)), ((), ())), preferred_element_type=F32)
        t = lax.broadcasted_iota(jnp.int32, sn.shape, 0) % dec_seq
        col = lax.broadcasted_iota(jnp.int32, sn.shape, 1)
        sn = jnp.where(col <= t, sn, NEG_INF)
        _softmax_update(sn, vn_ref[...], m_sc, l_sc, acc_sc)

        lam = _lambda(lam_ref, lam_init)
        half = rows_per_head // 2
        for h in range(N_KV_HEADS):
            r0 = h * rows_per_head
            blk = acc_sc[r0:r0 + rows_per_head, h * V_DIM:(h + 1) * V_DIM] / l_sc[r0:r0 + rows_per_head, :]
            o = blk[0:half] - lam * blk[half:rows_per_head]
            o_ref[h] = _subln(o, g_ref[...], lam_init)


def _attn_sample(page_table, qbd, kn, vn, lam_vecs, subln_g, cache_k, cache_v, *, layer, pages, dec_seq, lam_init):
    nb, n_pages = page_table.shape
    page = cache_k.shape[2]
    kvw = N_KV_HEADS * V_DIM
    nrow = qbd.shape[1]
    assert n_pages % pages == 0
    ck = cache_k.reshape(cache_k.shape[0], cache_k.shape[1], page * N_KV_HEADS, V_DIM)
    cv = cache_v.reshape(cache_v.shape[0], cache_v.shape[1], page * N_KV_HEADS, V_DIM)

    def page_spec(j):
        return pl.BlockSpec((None, None, page * N_KV_HEADS, V_DIM),
                            lambda b, s, pt: (layer, pt[b * n_pages + s * pages + j], 0, 0))

    fixed = lambda shape: pl.BlockSpec(shape, lambda b, s, pt: (0,) * len(shape))
    per_b = lambda shape: pl.BlockSpec((None,) + shape, lambda b, s, pt: (b,) + (0,) * len(shape))
    grid_spec = pltpu.PrefetchScalarGridSpec(
        num_scalar_prefetch=1,
        grid=(nb, n_pages // pages),
        in_specs=[per_b((nrow, kvw)), per_b((page, kvw)), per_b((page, kvw)),
                  fixed(lam_vecs.shape), fixed(subln_g.shape)]
                 + [page_spec(j) for j in range(pages)] * 2,
        out_specs=per_b((N_KV_HEADS, nrow // N_KV_HEADS // 2, V_DIM)),
        scratch_shapes=[pltpu.VMEM((nrow, 1), F32), pltpu.VMEM((nrow, 1), F32), pltpu.VMEM((nrow, kvw), F32)],
    )
    return pl.pallas_call(
        functools.partial(_attn_sample_kernel, pages=pages, dec_seq=dec_seq, lam_init=lam_init),
        grid_spec=grid_spec,
        out_shape=jax.ShapeDtypeStruct((nb, N_KV_HEADS, nrow // N_KV_HEADS // 2, V_DIM), F32),
        compiler_params=_cparams(("arbitrary", "arbitrary")),
        name="attn_sample",
    )(page_table.reshape(-1), qbd, kn, vn, lam_vecs, subln_g, *([ck] * pages), *([cv] * pages))


def _outproj_kernel(*refs, moe):
    if moe:
        (x_ref, mconv_ref, gate_ref, ya_ref, g1_ref, sh2_ref, sc2_ref, ng_ref, wba_ref, wo_ref, wr_ref,
         x1_ref, h2_ref, comb_ref) = refs
    else:
        (x_ref, mconv_ref, gate_ref, ya_ref, g1_ref, sh2_ref, sc2_ref, ng_ref, wba_ref, wo_ref,
         x1_ref, h2_ref) = refs
    att = jnp.dot(ya_ref[...], wba_ref[...], preferred_element_type=F32)
    m = mconv_ref[...].astype(F32) + gate_ref[...].astype(F32) * att
    x1 = x_ref[...] + g1_ref[...] * jnp.dot(m.astype(BF16), wo_ref[...], preferred_element_type=F32)
    x1_ref[...] = x1
    h2 = _rmsnorm(x1, ng_ref[...]) * (1.0 + sc2_ref[...]) + sh2_ref[...]
    h2_ref[...] = h2.astype(BF16)
    if moe:
        logits = lax.dot_general(wr_ref[...], h2, (((1,), (1,)), ((), ())), preferred_element_type=F32,
                                 precision=lax.Precision.HIGHEST)
        ne = logits.shape[0]
        idx = lax.broadcasted_iota(jnp.int32, logits.shape, 0)
        v1 = jnp.max(logits, axis=0, keepdims=True)
        i1 = jnp.min(jnp.where(logits == v1, idx, ne), axis=0, keepdims=True)
        rest = jnp.where(idx == i1, NEG_INF, logits)
        v2 = jnp.max(rest, axis=0, keepdims=True)
        i2 = jnp.min(jnp.where(rest == v2, idx, ne), axis=0, keepdims=True)
        e2 = jnp.exp(v2 - v1)
        den = 1.0 + e2
        comb_ref[...] = jnp.where(idx == i1, 1.0 / den, 0.0) + jnp.where(idx == i2, e2 / den, 0.0)


def _outproj(x, mconv, gate, ya, mod, ng, w_ba, w_o, w_router, *, tm, seq_len):
    n, d = x.shape
    tiles_per_seq = max(seq_len // tm, 1)
    rows = mod.shape[1]
    moe = w_router is not None
    row = lambda i: (i, 0)
    tile = pl.BlockSpec((tm, d), row)
    in_specs = [tile, tile, tile, tile,
                _mod_spec(rows, tiles_per_seq, 2, d), _mod_spec(rows, tiles_per_seq, 3, d),
                _mod_spec(rows, tiles_per_seq, 4, d), _resident((1, d)), _resident(w_ba.shape), _resident(w_o.shape)]
    args = [x, mconv, gate, ya, mod, mod, mod, ng, w_ba, w_o]
    out_shape = [jax.ShapeDtypeStruct((n, d), F32), jax.ShapeDtypeStruct((n, d), BF16)]
    out_specs = [tile, tile]
    if moe:
        ne = w_router.shape[0]
        in_specs.append(_resident(w_router.shape))
        args.append(w_router)
        out_shape.append(jax.ShapeDtypeStruct((ne, n), F32))
        out_specs.append(pl.BlockSpec((ne, tm), lambda i: (0, i)))
    return pl.pallas_call(
        functools.partial(_outproj_kernel, moe=moe),
        grid=(n // tm,), in_specs=in_specs, out_specs=out_specs, out_shape=out_shape,
        compiler_params=_cparams(("arbitrary",)),
        name="outproj",
    )(*args)


def _swiglu(x, wg_ref, wu_ref, wd_ref):
    gate = jnp.dot(x, wg_ref[...], preferred_element_type=F32)
    up = jnp.dot(x, wu_ref[...], preferred_element_type=F32)
    act = (gate * jax.nn.sigmoid(gate) * up).astype(BF16)
    return jnp.dot(act, wd_ref[...], preferred_element_type=F32)


def _residual_out(x1_ref, g2_ref, f, fg_ref):
    x2 = x1_ref[...] + g2_ref[...] * f
    return x2 if fg_ref is None else _rmsnorm(x2, fg_ref[...])


def _ffn_kernel(*refs, final):
    if final:
        x1_ref, h2_ref, g2_ref, wg_ref, wu_ref, wd_ref, fg_ref, o_ref, acc_sc = refs
    else:
        x1_ref, h2_ref, g2_ref, wg_ref, wu_ref, wd_ref, o_ref, acc_sc = refs
        fg_ref = None
    j = pl.program_id(1)

    @pl.when(j == 0)
    def _():
        acc_sc[...] = jnp.zeros(acc_sc.shape, F32)

    acc_sc[...] += _swiglu(h2_ref[...], wg_ref, wu_ref, wd_ref)

    @pl.when(j == pl.num_programs(1) - 1)
    def _():
        o_ref[...] = _residual_out(x1_ref, g2_ref, acc_sc[...], fg_ref)


def _ffn(x1, h2, mod, wg, wu, wd, final_g, *, tm, tf, seq_len):
    n, d = x1.shape
    dff = wg.shape[1]
    tiles_per_seq = max(seq_len // tm, 1)
    rows = mod.shape[1]
    final = final_g is not None
    tile = pl.BlockSpec((tm, d), lambda i, j: (i, 0))
    in_specs = [tile, tile, pl.BlockSpec((None, rows, d), lambda i, j: (i // tiles_per_seq, 0, 5)),
                pl.BlockSpec((d, tf), lambda i, j: (0, j)), pl.BlockSpec((d, tf), lambda i, j: (0, j)),
                pl.BlockSpec((tf, d), lambda i, j: (j, 0))]
    args = [x1, h2, mod, wg, wu, wd]
    if final:
        in_specs.append(pl.BlockSpec((1, d), lambda i, j: (0, 0)))
        args.append(final_g)
    return pl.pallas_call(
        functools.partial(_ffn_kernel, final=final),
        grid=(n // tm, dff // tf), in_specs=in_specs, out_specs=tile,
        out_shape=jax.ShapeDtypeStruct((n, d), F32),
        scratch_shapes=[pltpu.VMEM((tm, d), F32)],
        compiler_params=_cparams(("arbitrary", "arbitrary")),
        name="ffn",
    )(*args)


MOE_CHUNK = 128


def _moe_kernel(*refs, final, nj):
    if final:
        (x1_ref, h2_ref, g2_ref, comb_ref, tri_ref, wg_ref, wu_ref, wd_ref, fg_ref,
         o_ref, rank_sc, xc_sc, y_sc, nch_sm) = refs
    else:
        (x1_ref, h2_ref, g2_ref, comb_ref, tri_ref, wg_ref, wu_ref, wd_ref,
         o_ref, rank_sc, xc_sc, y_sc, nch_sm) = refs
        fg_ref = None
    e = pl.program_id(1)
    j = pl.program_id(2)
    ne = pl.num_programs(1)
    tb = h2_ref.shape[0]

    @pl.when(jnp.logical_and(e == 0, j == 0))
    def _():
        routed = comb_ref[...] > 0.0
        cnt = jnp.dot(jnp.where(routed, 1.0, 0.0).astype(BF16), tri_ref[...], preferred_element_type=F32)
        rank_sc[...] = jnp.where(routed, cnt, -1.0)
        o_ref[...] = jnp.zeros(o_ref.shape, F32)

    rank = rank_sc[pl.ds(e, 1), :]

    def one_hot(c):
        slot = (lax.broadcasted_iota(jnp.int32, (MOE_CHUNK, tb), 0) + c * MOE_CHUNK).astype(F32)
        return rank == slot

    @pl.when(j == 0)
    def _():
        n_tok = jnp.sum(jnp.where(rank >= 0.0, 1.0, 0.0)).astype(jnp.int32)
        nch = (n_tok + (MOE_CHUNK - 1)) // MOE_CHUNK
        nch_sm[0] = nch

        def body(c, carry):
            r0 = pl.multiple_of(c * MOE_CHUNK, MOE_CHUNK)
            sel = jnp.where(one_hot(c), 1.0, 0.0).astype(BF16)
            xc = jnp.dot(sel, h2_ref[...], preferred_element_type=F32).astype(BF16)
            xc_sc[pl.ds(r0, MOE_CHUNK), :] = xc
            y_sc[pl.ds(r0, MOE_CHUNK), :] = _swiglu(xc, wg_ref, wu_ref, wd_ref)
            return carry

        lax.fori_loop(0, nch, body, 0)

    if nj > 2:
        @pl.when(jnp.logical_and(j > 0, j < nj - 1))
        def _():
            def body(c, carry):
                r0 = pl.multiple_of(c * MOE_CHUNK, MOE_CHUNK)
                y_sc[pl.ds(r0, MOE_CHUNK), :] += _swiglu(xc_sc[pl.ds(r0, MOE_CHUNK), :], wg_ref, wu_ref, wd_ref)
                return carry

            lax.fori_loop(0, nch_sm[0], body, 0)

    @pl.when(j == nj - 1)
    def _():
        w_e = comb_ref[pl.ds(e, 1), :]

        def body(c, carry):
            r0 = pl.multiple_of(c * MOE_CHUNK, MOE_CHUNK)
            y = y_sc[pl.ds(r0, MOE_CHUNK), :] + _swiglu(xc_sc[pl.ds(r0, MOE_CHUNK), :], wg_ref, wu_ref, wd_ref)
            sel = jnp.where(one_hot(c), 1.0, 0.0)
            w_row = jnp.sum(sel * w_e, axis=1, keepdims=True)
            yw = (y * w_row).astype(BF16)
            o_ref[...] += jnp.dot(sel.T.astype(BF16), yw, preferred_element_type=F32)
            return carry

        lax.fori_loop(0, nch_sm[0], body, 0)

    @pl.when(jnp.logical_and(e == ne - 1, j == nj - 1))
    def _():
        o_ref[...] = _residual_out(x1_ref, g2_ref, o_ref[...], fg_ref)


def _moe(x1, h2, mod, comb_t, wg, wu, wd, final_g, *, tb, tf, seq_len):
    n, d = x1.shape
    ne, _, dff = wg.shape
    nj = dff // tf
    assert nj >= 2 and tb % MOE_CHUNK == 0 and n % tb == 0
    blocks_per_seq = max(seq_len // tb, 1)
    rows = mod.shape[1]
    final = final_g is not None
    pos = jnp.arange(tb)
    tri = (pos[:, None] < pos[None, :]).astype(BF16)
    tile = lambda i, e, j: (i, 0)
    in_specs = [pl.BlockSpec((tb, d), tile, pipeline_mode=pl.Buffered(1)),
                pl.BlockSpec((tb, d), tile),
                pl.BlockSpec((None, rows, d), lambda i, e, j: (i // blocks_per_seq, 0, 5)),
                pl.BlockSpec((ne, tb), lambda i, e, j: (0, i)),
                pl.BlockSpec((tb, tb), lambda i, e, j: (0, 0), pipeline_mode=pl.Buffered(1)),
                pl.BlockSpec((None, d, tf), lambda i, e, j: (e, 0, j)),
                pl.BlockSpec((None, d, tf), lambda i, e, j: (e, 0, j)),
                pl.BlockSpec((None, tf, d), lambda i, e, j: (e, j, 0))]
    args = [x1, h2, mod, comb_t, tri, wg, wu, wd]
    if final:
        in_specs.append(pl.BlockSpec((1, d), lambda i, e, j: (0, 0)))
        args.append(final_g)
    return pl.pallas_call(
        functools.partial(_moe_kernel, final=final, nj=nj),
        grid=(n // tb, ne, nj), in_specs=in_specs, out_specs=pl.BlockSpec((tb, d), tile),
        out_shape=jax.ShapeDtypeStruct((n, d), F32),
        scratch_shapes=[pltpu.VMEM((ne, tb), F32), pltpu.VMEM((tb, d), BF16), pltpu.VMEM((tb, d), F32),
                        pltpu.SMEM((1,), jnp.int32)],
        compiler_params=_cparams(("arbitrary", "arbitrary", "arbitrary")),
        name="moe",
    )(*args)


def _rope_tables(pos):
    half = ROT_DIM // 2
    inv = ROPE_THETA ** (-jnp.arange(half, dtype=F32) / half)
    ang = pos.astype(F32)[:, None] * inv[None, :]
    cos, sin = jnp.cos(ang), jnp.sin(ang)
    r = jnp.arange(LANES) % HEAD_DIM
    idx = r % half
    lo = (r < half)[None, :]
    hi = ((r >= half) & (r < ROT_DIM))[None, :]
    c = jnp.where((r < ROT_DIM)[None, :], cos[:, idx], 1.0)
    sa = jnp.where(lo, -sin[:, idx], 0.0)
    sb = jnp.where(hi, sin[:, idx], 0.0)
    return c, sa, sb


def _pick_ff_tile(dff, cap):
    best = None
    for t in range(LANES, cap + 1, LANES):
        if dff % t == 0:
            best = t
    return best if best is not None else dff


def kernel(x_prompt, x_sample, cache_k, cache_v, state_conv, page_table, c_prompt, c_sample, w_ada, b_ada, norm1_g, norm2_g, w_in, conv_w, lambda_q1, lambda_k1, lambda_q2, lambda_k2, subln_g, w_br_conv, w_br_attn, w_o, w_ff_gate, w_ff_up, w_ff_down, w_router, w_moe_gate, w_moe_up, w_moe_down, final_g):
    nseq, seq_len, d = x_prompt.shape
    nb, dec_seq, _ = x_sample.shape
    depth = w_ada.shape[0]
    page = cache_k.shape[2]
    past_len = page_table.shape[1] * page
    kvw = N_KV_HEADS * V_DIM
    n_p, n_s = nseq * seq_len, nb * dec_seq
    tm_p = min(512, seq_len)
    tq = min(1024, seq_len)
    tk = min(512, seq_len)
    tb_p = min(1024, seq_len)
    assert dec_seq == 8 and d % LANES == 0 and n_s % MOE_CHUNK == 0

    mod_all = _ada(jnp.concatenate([c_prompt, c_sample], axis=0), w_ada, b_ada)
    tabs_p = _rope_tables(jnp.arange(seq_len))
    tabs_s = tuple(jnp.tile(t, (nb, 1)) for t in _rope_tables(past_len + jnp.arange(dec_seq)))
    final_row = final_g.reshape(1, d)
    eye_kv = jnp.eye(N_KV_HEADS, dtype=BF16)

    xp = x_prompt.reshape(n_p, d)
    xs = x_sample.reshape(n_s, d)
    outs = [[] for _ in range(6)]
    new_kv_p = None
    for l in range(depth):
        lam_init = 0.8 - 0.6 * math.exp(-0.3 * l)
        mod_p = mod_all[l, :nseq].reshape(nseq, 1, 6 * d)
        mod_s = jnp.repeat(mod_all[l, nseq:], dec_seq, axis=0).reshape(1, n_s, 6 * d)
        w_in_l = w_in[l].astype(BF16)
        w_brc = w_br_conv[l].astype(BF16)
        w_bra = w_br_attn[l].astype(BF16)
        w_o_l = w_o[l].astype(BF16)
        ng1 = norm1_g[l].reshape(1, d)
        ng2 = norm2_g[l].reshape(1, d)
        lam_vecs = jnp.stack([lambda_q1[l], lambda_k1[l], lambda_q2[l], lambda_k2[l]])
        sg = subln_g[l].reshape(1, V_DIM)
        st = state_conv[l]
        state = (jnp.repeat(st[:, 0], dec_seq, axis=0), jnp.repeat(st[:, 1], dec_seq, axis=0))
        moe = l % 2 == 1
        i = l // 2
        if moe:
            wr = w_router[i].T
            wg, wu, wd = w_moe_gate[i].astype(BF16), w_moe_up[i].astype(BF16), w_moe_down[i].astype(BF16)
        else:
            wr = None
            wg, wu, wd = w_ff_gate[i].astype(BF16), w_ff_up[i].astype(BF16), w_ff_down[i].astype(BF16)
        tf = _pick_ff_tile(wg.shape[-1], 1792)
        fin = final_row if l == depth - 1 else None

        def mixer(res, mod, tm, tb, seq):
            if moe:
                return _moe(res[0], res[1], mod, res[2], wg, wu, wd, fin, tb=tb, tf=tf, seq_len=seq)
            return _ffn(res[0], res[1], mod, wg, wu, wd, fin, tm=tm, tf=tf, seq_len=seq)

        mconv, gate, q, kf, vf, kb, vt, cs = _inproj(
            xp, mod_p, ng1, w_in_l, w_brc, conv_w[l], tabs_p, None, tm=tm_p, seq_len=seq_len, sample=False, tk=tk,
            layer=l, depth=depth, carried=new_kv_p)
        new_kv_p = (kf, vf)
        ya = _attn_prompt(q, kb, vt, lam_vecs, sg.reshape(V_DIM, 1), nseq=nseq, seq_len=seq_len, tq=tq,
                          lam_init=lam_init)
        res = _outproj(xp, mconv, gate, ya, mod_p, ng2, w_bra, w_o_l, wr, tm=tm_p, seq_len=seq_len)
        xp = mixer(res, mod_p, tm_p, tb_p, seq_len)
        outs[2].append(cs)

        mconv, gate, q, kf, vf, kb, vb, u = _inproj(
            xs, mod_s, ng1, w_in_l, w_brc, conv_w[l], tabs_s, state, tm=n_s, seq_len=n_s, sample=True)
        qe = q.reshape(N_KV_HEADS, GROUP, 2, nb, dec_seq, LANES).transpose(3, 0, 2, 1, 4, 5)
        qbd = (qe[:, :, :, :, :, None, :] * eye_kv[None, :, None, None, None, :, None]).reshape(
            nb, N_KV_HEADS * 2 * GROUP * dec_seq, kvw)
        pad = ((0, 0), (0, page - dec_seq), (0, 0))
        kn = jnp.pad(kb.reshape(nb, dec_seq, kvw), pad)
        vn = jnp.pad(vb.reshape(nb, dec_seq, kvw), pad)
        o = _attn_sample(page_table, qbd, kn, vn, lam_vecs, sg, cache_k, cache_v,
                         layer=l, pages=min(32, page_table.shape[1]), dec_seq=dec_seq, lam_init=lam_init)
        ya = o.reshape(nb, N_KV_HEADS, GROUP, dec_seq, V_DIM).transpose(0, 3, 1, 2, 4).reshape(n_s, d).astype(BF16)
        res = _outproj(xs, mconv, gate, ya, mod_s, ng2, w_bra, w_o_l, wr, tm=n_s, seq_len=n_s)
        xs = mixer(res, mod_s, n_s, n_s, n_s)
        outs[3].append(kf.reshape(nb, dec_seq, N_KV_HEADS, V_DIM))
        outs[4].append(vf.reshape(nb, dec_seq, N_KV_HEADS, V_DIM))
        outs[5].append(u.reshape(nb, dec_seq, d)[:, dec_seq - 2:])

    kv_shape = (depth, nseq, seq_len, N_KV_HEADS, V_DIM)
    return (xp.reshape(nseq, seq_len, d), xs.reshape(nb, dec_seq, d),
            new_kv_p[0].reshape(kv_shape), new_kv_p[1].reshape(kv_shape), jnp.stack(outs[2]),
            jnp.stack(outs[3]), jnp.stack(outs[4]), jnp.stack(outs[5]))
```

```python
import functools
import math

import jax
import jax.numpy as jnp
from jax import lax
from jax.experimental import pallas as pl
from jax.experimental.pallas import tpu as pltpu

F32 = jnp.float32
BF16 = jnp.bfloat16

N_HEADS = 8
N_KV_HEADS = 4
GROUP = N_HEADS // N_KV_HEADS
HEAD_DIM = 64
V_DIM = 2 * HEAD_DIM
ROT_DIM = HEAD_DIM // 4
ROPE_THETA = 500000.0
TOP_K = 2
EPS = 1e-6

LANES = 128
V7X_VMEM_BYTES = 64 * 2 ** 20
VMEM_LIMIT = V7X_VMEM_BYTES - 8 * 2 ** 20

NEG_INF = float("-inf")
LOG2E = math.log2(math.e)


def _cparams(sem):
    return pltpu.CompilerParams(dimension_semantics=sem, vmem_limit_bytes=VMEM_LIMIT)


def _resident(shape):
    nd = len(shape)
    return pl.BlockSpec(shape, lambda *_: (0,) * nd, pipeline_mode=pl.Buffered(1))


def _mod_spec(rows, tiles_per_group, piece, d):
    return pl.BlockSpec((None, rows, d), lambda i: (i // tiles_per_group, 0, piece))


def _rmsnorm(x, g):
    return x * lax.rsqrt(jnp.mean(x * x, axis=-1, keepdims=True) + EPS) * g


def _ada_kernel(c_ref, w_ref, b_ref, o_ref):
    c = c_ref[...]
    a = (c * jax.nn.sigmoid(c)).astype(BF16)
    o_ref[...] = jnp.dot(a, w_ref[...].astype(BF16), preferred_element_type=F32) + b_ref[...]


def _ada(c_all, w_ada, b_ada):
    depth, d, d6 = w_ada.shape
    n = c_all.shape[0]
    return pl.pallas_call(
        _ada_kernel,
        grid=(depth, d6 // d),
        in_specs=[pl.BlockSpec((n, d), lambda l, j: (0, 0)),
                  pl.BlockSpec((None, d, d), lambda l, j: (l, 0, j)),
                  pl.BlockSpec((None, 1, d), lambda l, j: (l, 0, j))],
        out_specs=pl.BlockSpec((None, n, d), lambda l, j: (l, 0, j)),
        out_shape=jax.ShapeDtypeStruct((depth, n, d6), F32),
        compiler_params=_cparams(("arbitrary", "arbitrary")),
        name="ada",
    )(c_all, w_ada, b_ada.reshape(depth, 1, d6))


def _rope_chunk(xc, cos, sa, sb):
    half = ROT_DIM // 2
    return xc * cos + pltpu.roll(xc, LANES - half, 1) * sa + pltpu.roll(xc, half, 1) * sb


def _inproj_kernel(*refs, tm, d, sample, tiles_per_seq, n_carried):
    if sample:
        (x_ref, sh_ref, sc_ref, ng_ref, win_ref, wbr_ref, cw_ref, cos_ref, sa_ref, sb_ref, st0_ref, st1_ref,
         mconv_ref, gate_ref, q_ref, kf_ref, vf_ref, kb_ref, vb_ref, cs_ref, ubuf) = refs
    else:
        (x_ref, sh_ref, sc_ref, ng_ref, win_ref, wbr_ref, cw_ref, cos_ref, sa_ref, sb_ref) = refs[:10]
        (mconv_ref, gate_ref, q_ref, kf_ref, vf_ref, kb_ref, vt_ref, cs_ref, ubuf) = refs[10 + n_carried:]
    i = pl.program_id(0)
    kvw = N_KV_HEADS * V_DIM
    o_q = 3 * d
    o_k = o_q + N_HEADS * 2 * HEAD_DIM
    o_v = o_k + kvw
    o_ga = o_v + kvw
    o_gb = o_ga + d

    h = (_rmsnorm(x_ref[...], ng_ref[...]) * (1.0 + sc_ref[...]) + sh_ref[...]).astype(BF16)

    def proj(a, b):
        return jnp.dot(h, win_ref[:, a:b], preferred_element_type=F32)

    u = proj(d, 2 * d) * proj(2 * d, 3 * d)
    if sample:
        ubuf[0:8, :] = jnp.zeros((8, d), F32)
    else:
        @pl.when(i % tiles_per_seq == 0)
        def _():
            ubuf[0:8, :] = jnp.zeros((8, d), F32)
    ubuf[8:tm + 8, :] = u
    up1 = ubuf[7:tm + 7, :]
    up2 = ubuf[6:tm + 6, :]
    if sample:
        t = lax.broadcasted_iota(jnp.int32, (tm, d), 0) & 7
        st0 = st0_ref[...]
        st1 = st1_ref[...]
        up1 = jnp.where(t >= 1, up1, st1)
        up2 = jnp.where(t >= 2, up2, jnp.where(t == 1, st1, st0))
        cs_ref[...] = u
    else:
        cs_ref[...] = ubuf[tm + 6:tm + 8, :]
        ubuf[0:8, :] = ubuf[tm:tm + 8, :]
    cu = cw_ref[0:1, :] * up2 + cw_ref[1:2, :] * up1 + cw_ref[2:3, :] * u
    y_conv = (proj(0, d) * cu).astype(BF16)
    yc = jnp.dot(y_conv, wbr_ref[...], preferred_element_type=F32)
    mconv_ref[...] = (jax.nn.sigmoid(proj(o_ga, o_gb)) * yc).astype(BF16)
    gate_ref[...] = jax.nn.sigmoid(proj(o_gb, o_gb + d)).astype(BF16)

    cos = cos_ref[...]
    sa = sa_ref[...]
    sb = sb_ref[...]
    lane = lax.broadcasted_iota(jnp.int32, (tm, LANES), 1)
    zq = proj(o_q, o_k) * (HEAD_DIM ** -0.5 * LOG2E)
    for hq in range(N_HEADS):
        r = _rope_chunk(zq[:, hq * LANES:(hq + 1) * LANES], cos, sa, sb)
        kv, g = hq // GROUP, hq % GROUP
        q_ref[kv, 2 * g] = jnp.where(lane < HEAD_DIM, r, 0.0).astype(BF16)
        q_ref[kv, 2 * g + 1] = jnp.where(lane >= HEAD_DIM, r, 0.0).astype(BF16)
    zk = proj(o_k, o_v)
    zv = proj(o_v, o_ga)
    for kv in range(N_KV_HEADS):
        r = _rope_chunk(zk[:, kv * LANES:(kv + 1) * LANES], cos, sa, sb)
        kb_ref[:, kv * LANES:(kv + 1) * LANES] = r.astype(BF16)
        if sample:
            kf_ref[:, kv * LANES:(kv + 1) * LANES] = r
        else:
            kf_ref[pl.ds(kv, tm, stride=N_KV_HEADS), :] = r
            vf_ref[pl.ds(kv, tm, stride=N_KV_HEADS), :] = zv[:, kv * V_DIM:(kv + 1) * V_DIM]
    if sample:
        vf_ref[...] = zv
        vb_ref[...] = zv.astype(BF16)
    else:
        tk = vt_ref.shape[-1]
        for c in range(tm // tk):
            for kv in range(N_KV_HEADS):
                vt_ref[c, kv] = zv[c * tk:(c + 1) * tk, kv * V_DIM:(kv + 1) * V_DIM].T.astype(BF16)


def _inproj(x, mod, ng, w_in, w_br, cw, tabs, state, *, tm, seq_len, sample, tk=None, layer=0, depth=1,
            carried=None):
    n, d = x.shape
    nt = n // tm
    kvw = N_KV_HEADS * V_DIM
    tiles_per_seq = max(seq_len // tm, 1)
    table_tiles = tabs[0].shape[0] // tm
    rows = mod.shape[1]
    row = lambda i: (i, 0)
    in_specs = [pl.BlockSpec((tm, d), row),
                _mod_spec(rows, tiles_per_seq, 0, d), _mod_spec(rows, tiles_per_seq, 1, d),
                _resident((1, d)), _resident(w_in.shape), _resident(w_br.shape), _resident(cw.shape)]
    in_specs += [pl.BlockSpec((tm, LANES), lambda i: (i % table_tiles, 0))] * 3
    args = [x, mod, mod, ng, w_in, w_br, cw, *tabs]
    aliases = {}
    if sample:
        in_specs += [pl.BlockSpec((tm, d), row)] * 2
        args += list(state)
        cs_shape, cs_spec = (n, d), pl.BlockSpec((tm, d), row)
        v_shape, v_spec = (n, kvw), pl.BlockSpec((tm, kvw), row)
        kv_shape, kv_spec = (n, kvw), pl.BlockSpec((tm, kvw), row)
    else:
        nseq = n // seq_len
        cs_shape, cs_spec = (nseq, 2, d), pl.BlockSpec((None, 2, d), lambda i: (i // tiles_per_seq, 0, 0))
        v_shape = (n // tk, N_KV_HEADS, V_DIM, tk)
        v_spec = pl.BlockSpec((tm // tk, N_KV_HEADS, V_DIM, tk), lambda i: (i, 0, 0, 0))
        kv_shape = (depth, n * N_KV_HEADS, V_DIM)
        kv_spec = pl.BlockSpec((None, tm * N_KV_HEADS, V_DIM), lambda i: (layer, i, 0))
        if carried is not None:
            aliases = {len(args): 3, len(args) + 1: 4}
            in_specs += [pl.BlockSpec(memory_space=pl.ANY)] * 2
            args += list(carried)
    out_shape = [jax.ShapeDtypeStruct((n, d), BF16), jax.ShapeDtypeStruct((n, d), BF16),
                 jax.ShapeDtypeStruct((N_KV_HEADS, 2 * GROUP, n, LANES), BF16),
                 jax.ShapeDtypeStruct(kv_shape, F32), jax.ShapeDtypeStruct(kv_shape, F32),
                 jax.ShapeDtypeStruct((n, kvw), BF16), jax.ShapeDtypeStruct(v_shape, BF16),
                 jax.ShapeDtypeStruct(cs_shape, F32)]
    out_specs = [pl.BlockSpec((tm, d), row), pl.BlockSpec((tm, d), row),
                 pl.BlockSpec((N_KV_HEADS, 2 * GROUP, tm, LANES), lambda i: (0, 0, i, 0)),
                 kv_spec, kv_spec,
                 pl.BlockSpec((tm, kvw), row), v_spec, cs_spec]
    return pl.pallas_call(
        functools.partial(_inproj_kernel, tm=tm, d=d, sample=sample, tiles_per_seq=tiles_per_seq,
                          n_carried=len(aliases)),
        grid=(nt,), in_specs=in_specs, out_specs=out_specs, out_shape=out_shape,
        scratch_shapes=[pltpu.VMEM((tm + 8, d), F32)],
        input_output_aliases=aliases,
        compiler_params=_cparams(("arbitrary",)),
        name="inproj_sample" if sample else "inproj_prompt",
    )(*args)


def _lambda(lam_ref, lam_init):
    a = jnp.sum(lam_ref[0:1, :] * lam_ref[1:2, :], axis=-1, keepdims=True)
    b = jnp.sum(lam_ref[2:3, :] * lam_ref[3:4, :], axis=-1, keepdims=True)
    return jnp.exp(a) - jnp.exp(b) + lam_init


def _softmax_update(s, v, m_sc, l_sc, acc_sc):
    m_prev = m_sc[...]
    m_new = jnp.maximum(m_prev, jnp.max(s, axis=-1, keepdims=True))
    alpha = jnp.exp2(m_prev - m_new)
    p = jnp.exp2(s - m_new)
    l_sc[...] = alpha * l_sc[...] + jnp.sum(p, axis=-1, keepdims=True)
    acc_sc[...] = alpha * acc_sc[...] + jnp.dot(p.astype(BF16), v, preferred_element_type=F32)
    m_sc[...] = m_new


def _subln(o, g, lam_init):
    return _rmsnorm(o, g) * (1.0 - lam_init)


AHEAD = 2
ONES_ROWS = 16


def _attn_prompt_kernel(q_ref, k_ref, vt_ref, lam_ref, g_ref, o_ref, m_sc, acc_sc, *, tq, lam_init):
    qi = pl.program_id(2)
    nslab = 2 * GROUP
    tk = vt_ref.shape[-1]
    ratio = tq // tk
    strip = tk
    m_sc[...] = jnp.full(m_sc.shape, NEG_INF, F32)
    acc_sc[...] = jnp.zeros(acc_sc.shape, F32)

    def run_tiles(tiles):
        items = [(t, sl, c) for t, (_, diag) in enumerate(tiles) for sl in range(nslab)
                 for c in range(tq // strip) if diag is None or diag * tk < (c + 1) * strip]
        ks = [k_ref[pl.ds(pl.multiple_of(ki * tk, tk), tk), :] for ki, _ in tiles]

        def scores(n):
            t, sl, c = items[n]
            return lax.dot_general(ks[t], q_ref[sl, c * strip:(c + 1) * strip, :], (((1,), (1,)), ((), ())),
                                   preferred_element_type=F32)

        pending = [scores(n) for n in range(min(AHEAD, len(items)))]
        for n, (t, sl, c) in enumerate(items):
            ki, diag = tiles[t]
            cols = slice(c * strip, (c + 1) * strip)
            st = pending.pop(0)
            if n + AHEAD < len(items):
                pending.append(scores(n + AHEAD))
            if diag is not None and (diag + 1) * tk > c * strip + 1:
                key = lax.broadcasted_iota(jnp.int32, (tk, strip), 0) + diag * tk
                tok = lax.broadcasted_iota(jnp.int32, (tk, strip), 1) + c * strip
                st = jnp.where(key <= tok, st, NEG_INF)
            m_prev = m_sc[sl, :, cols]
            m_new = jnp.maximum(m_prev, jnp.max(st, axis=0, keepdims=True))
            alpha = jnp.exp2(m_prev - m_new)
            p = jnp.exp2(st - m_new).astype(BF16)
            vt1 = jnp.concatenate([vt_ref[ki], jnp.ones((ONES_ROWS, tk), BF16)], axis=0)
            acc_sc[sl, :, cols] = alpha * acc_sc[sl, :, cols] + jnp.dot(vt1, p, preferred_element_type=F32)
            m_sc[sl, :, cols] = m_new

    def body(kq, carry):
        run_tiles([(kq * ratio + r, None) for r in range(ratio)])
        return carry

    lax.fori_loop(0, qi, body, 0)
    run_tiles([(qi * ratio + dg, dg) for dg in range(ratio)])

    lam = _lambda(lam_ref, lam_init)
    def normalised(sl):
        return acc_sc[sl, 0:V_DIM, :] * (1.0 / acc_sc[sl, V_DIM:V_DIM + 1, :])

    for g in range(GROUP):
        og = normalised(2 * g) - lam * normalised(2 * g + 1)
        ms = jnp.mean(og * og, axis=0, keepdims=True)
        y = og * lax.rsqrt(ms + EPS) * g_ref[...] * (1.0 - lam_init)
        o_ref[:, g * V_DIM:(g + 1) * V_DIM] = y.T.astype(BF16)


def _attn_prompt(q, kb, vt, lam_vecs, subln_col, *, nseq, seq_len, tq, lam_init):
    n = kb.shape[0]
    nq = seq_len // tq
    nslab = 2 * GROUP
    tk = vt.shape[-1]
    assert seq_len % tq == 0 and tq % tk == 0 and vt.shape == (n // tk, N_KV_HEADS, V_DIM, tk)
    return pl.pallas_call(
        functools.partial(_attn_prompt_kernel, tq=tq, lam_init=lam_init),
        grid=(nseq, N_KV_HEADS, nq),
        in_specs=[pl.BlockSpec((None, nslab, tq, LANES), lambda b, h, i: (h, 0, b * nq + i, 0)),
                  pl.BlockSpec((seq_len, V_DIM), lambda b, h, i: (b, h)),
                  pl.BlockSpec((seq_len // tk, None, V_DIM, tk), lambda b, h, i: (b, h, 0, 0)),
                  pl.BlockSpec(lam_vecs.shape, lambda b, h, i: (0, 0)),
                  pl.BlockSpec(subln_col.shape, lambda b, h, i: (0, 0))],
        out_specs=pl.BlockSpec((tq, GROUP * V_DIM), lambda b, h, i: (b * nq + i, h)),
        out_shape=jax.ShapeDtypeStruct((n, N_HEADS * V_DIM), BF16),
        scratch_shapes=[pltpu.VMEM((nslab, 1, tq), F32), pltpu.VMEM((nslab, V_DIM + ONES_ROWS, tq), F32)],
        compiler_params=_cparams(("arbitrary", "arbitrary", "arbitrary")),
        name="attn_prompt",
    )(q, kb, vt, lam_vecs, subln_col)


def _attn_sample_kernel(pt_ref, q_ref, kn_ref, vn_ref, lam_ref, g_ref, *refs, pages, dec_seq, lam_init):
    k_refs = refs[:pages]
    v_refs = refs[pages:2 * pages]
    o_ref, m_sc, l_sc, acc_sc = refs[2 * pages:]
    step = pl.program_id(1)
    q = q_ref[...]
    nrow = q.shape[0]
    rows_per_head = nrow // N_KV_HEADS

    @pl.when(step == 0)
    def _():
        m_sc[...] = jnp.full(m_sc.shape, NEG_INF, F32)
        l_sc[...] = jnp.zeros(l_sc.shape, F32)
        acc_sc[...] = jnp.zeros(acc_sc.shape, F32)

    def page_rows(ref):
        page = ref.shape[0] // N_KV_HEADS
        return jnp.concatenate([ref[pl.ds(h, page, stride=N_KV_HEADS), :] for h in range(N_KV_HEADS)],
                               axis=1).astype(BF16)

    k = jnp.concatenate([page_rows(r) for r in k_refs], axis=0)
    v = jnp.concatenate([page_rows(r) for r in v_refs], axis=0)
    s = lax.dot_general(q, k, (((1,), (1,)), ((), ())), preferred_element_type=F32)
    _softmax_update(s, v, m_sc, l_sc, acc_sc)

    @pl.when(step == pl.num_programs(1) - 1)
    def _():
        kn = kn_ref[...]
        sn = lax.dot_general(q, kn, (((1,), (1,)), ((), ())), preferred_element_type=F32)
        t = lax.broadcasted_iota(jnp.int32, sn.shape, 0) % dec_seq
        col = lax.broadcasted_iota(jnp.int32, sn.shape, 1)
        sn = jnp.where(col <= t, sn, NEG_INF)
        _softmax_update(sn, vn_ref[...], m_sc, l_sc, acc_sc)

        lam = _lambda(lam_ref, lam_init)
        half = rows_per_head // 2
        for h in range(N_KV_HEADS):
            r0 = h * rows_per_head
            blk = acc_sc[r0:r0 + rows_per_head, h * V_DIM:(h + 1) * V_DIM] / l_sc[r0:r0 + rows_per_head, :]
            o = blk[0:half] - lam * blk[half:rows_per_head]
            o_ref[h] = _subln(o, g_ref[...], lam_init)


def _attn_sample(page_table, qbd, kn, vn, lam_vecs, subln_g, cache_k, cache_v, *, layer, pages, dec_seq, lam_init):
    nb, n_pages = page_table.shape
    page = cache_k.shape[2]
    kvw = N_KV_HEADS * V_DIM
    nrow = qbd.shape[1]
    assert n_pages % pages == 0
    ck = cache_k.reshape(cache_k.shape[0], cache_k.shape[1], page * N_KV_HEADS, V_DIM)
    cv = cache_v.reshape(cache_v.shape[0], cache_v.shape[1], page * N_KV_HEADS, V_DIM)

    def page_spec(j):
        return pl.BlockSpec((None, None, page * N_KV_HEADS, V_DIM),
                            lambda b, s, pt: (layer, pt[b * n_pages + s * pages + j], 0, 0))

    fixed = lambda shape: pl.BlockSpec(shape, lambda b, s, pt: (0,) * len(shape))
    per_b = lambda shape: pl.BlockSpec((None,) + shape, lambda b, s, pt: (b,) + (0,) * len(shape))
    grid_spec = pltpu.PrefetchScalarGridSpec(
        num_scalar_prefetch=1,
        grid=(nb, n_pages // pages),
        in_specs=[per_b((nrow, kvw)), per_b((page, kvw)), per_b((page, kvw)),
                  fixed(lam_vecs.shape), fixed(subln_g.shape)]
                 + [page_spec(j) for j in range(pages)] * 2,
        out_specs=per_b((N_KV_HEADS, nrow // N_KV_HEADS // 2, V_DIM)),
        scratch_shapes=[pltpu.VMEM((nrow, 1), F32), pltpu.VMEM((nrow, 1), F32), pltpu.VMEM((nrow, kvw), F32)],
    )
    return pl.pallas_call(
        functools.partial(_attn_sample_kernel, pages=pages, dec_seq=dec_seq, lam_init=lam_init),
        grid_spec=grid_spec,
        out_shape=jax.ShapeDtypeStruct((nb, N_KV_HEADS, nrow // N_KV_HEADS // 2, V_DIM), F32),
        compiler_params=_cparams(("arbitrary", "arbitrary")),
        name="attn_sample",
    )(page_table.reshape(-1), qbd, kn, vn, lam_vecs, subln_g, *([ck] * pages), *([cv] * pages))


def _outproj_kernel(*refs, moe):
    if moe:
        (x_ref, mconv_ref, gate_ref, ya_ref, g1_ref, sh2_ref, sc2_ref, ng_ref, wba_ref, wo_ref, wr_ref,
         x1_ref, h2_ref, comb_ref) = refs
    else:
        (x_ref, mconv_ref, gate_ref, ya_ref, g1_ref, sh2_ref, sc2_ref, ng_ref, wba_ref, wo_ref,
         x1_ref, h2_ref) = refs
    att = jnp.dot(ya_ref[...], wba_ref[...], preferred_element_type=F32)
    m = mconv_ref[...].astype(F32) + gate_ref[...].astype(F32) * att
    x1 = x_ref[...] + g1_ref[...] * jnp.dot(m.astype(BF16), wo_ref[...], preferred_element_type=F32)
    x1_ref[...] = x1
    h2 = _rmsnorm(x1, ng_ref[...]) * (1.0 + sc2_ref[...]) + sh2_ref[...]
    h2_ref[...] = h2.astype(BF16)
    if moe:
        logits = lax.dot_general(wr_ref[...], h2, (((1,), (1,)), ((), ())), preferred_element_type=F32,
                                 precision=lax.Precision.HIGHEST)
        ne = logits.shape[0]
        idx = lax.broadcasted_iota(jnp.int32, logits.shape, 0)
        v1 = jnp.max(logits, axis=0, keepdims=True)
        i1 = jnp.min(jnp.where(logits == v1, idx, ne), axis=0, keepdims=True)
        rest = jnp.where(idx == i1, NEG_INF, logits)
        v2 = jnp.max(rest, axis=0, keepdims=True)
        i2 = jnp.min(jnp.where(rest == v2, idx, ne), axis=0, keepdims=True)
        e2 = jnp.exp(v2 - v1)
        den = 1.0 + e2
        comb_ref[...] = jnp.where(idx == i1, 1.0 / den, 0.0) + jnp.where(idx == i2, e2 / den, 0.0)


def _outproj(x, mconv, gate, ya, mod, ng, w_ba, w_o, w_router, *, tm, seq_len):
    n, d = x.shape
    tiles_per_seq = max(seq_len // tm, 1)
    rows = mod.shape[1]
    moe = w_router is not None
    row = lambda i: (i, 0)
    tile = pl.BlockSpec((tm, d), row)
    in_specs = [tile, tile, tile, tile,
                _mod_spec(rows, tiles_per_seq, 2, d), _mod_spec(rows, tiles_per_seq, 3, d),
                _mod_spec(rows, tiles_per_seq, 4, d), _resident((1, d)), _resident(w_ba.shape), _resident(w_o.shape)]
    args = [x, mconv, gate, ya, mod, mod, mod, ng, w_ba, w_o]
    out_shape = [jax.ShapeDtypeStruct((n, d), F32), jax.ShapeDtypeStruct((n, d), BF16)]
    out_specs = [tile, tile]
    if moe:
        ne = w_router.shape[0]
        in_specs.append(_resident(w_router.shape))
        args.append(w_router)
        out_shape.append(jax.ShapeDtypeStruct((ne, n), F32))
        out_specs.append(pl.BlockSpec((ne, tm), lambda i: (0, i)))
    return pl.pallas_call(
        functools.partial(_outproj_kernel, moe=moe),
        grid=(n // tm,), in_specs=in_specs, out_specs=out_specs, out_shape=out_shape,
        compiler_params=_cparams(("arbitrary",)),
        name="outproj",
    )(*args)


def _swiglu(x, wg_ref, wu_ref, wd_ref):
    gate = jnp.dot(x, wg_ref[...], preferred_element_type=F32)
    up = jnp.dot(x, wu_ref[...], preferred_element_type=F32)
    act = (gate * jax.nn.sigmoid(gate) * up).astype(BF16)
    return jnp.dot(act, wd_ref[...], preferred_element_type=F32)


def _residual_out(x1_ref, g2_ref, f, fg_ref):
    x2 = x1_ref[...] + g2_ref[...] * f
    return x2 if fg_ref is None else _rmsnorm(x2, fg_ref[...])


def _outproj_ffn_kernel(*refs, final):
    if final:
        (x_ref, mconv_ref, gate_ref, ya_ref, g1_ref, sh2_ref, sc2_ref, ng_ref, wba_ref, wo_ref, g2_ref,
         wg_ref, wu_ref, wd_ref, fg_ref, o_ref, x1_sc, h2_sc, acc_sc) = refs
    else:
        (x_ref, mconv_ref, gate_ref, ya_ref, g1_ref, sh2_ref, sc2_ref, ng_ref, wba_ref, wo_ref, g2_ref,
         wg_ref, wu_ref, wd_ref, o_ref, x1_sc, h2_sc, acc_sc) = refs
        fg_ref = None
    j = pl.program_id(1)

    @pl.when(j == 0)
    def _():
        att = jnp.dot(ya_ref[...], wba_ref[...], preferred_element_type=F32)
        m = mconv_ref[...].astype(F32) + gate_ref[...].astype(F32) * att
        x1 = x_ref[...] + g1_ref[...] * jnp.dot(m.astype(BF16), wo_ref[...], preferred_element_type=F32)
        x1_sc[...] = x1
        h2_sc[...] = (_rmsnorm(x1, ng_ref[...]) * (1.0 + sc2_ref[...]) + sh2_ref[...]).astype(BF16)
        acc_sc[...] = jnp.zeros(acc_sc.shape, F32)

    acc_sc[...] += _swiglu(h2_sc[...], wg_ref, wu_ref, wd_ref)

    @pl.when(j == pl.num_programs(1) - 1)
    def _():
        o_ref[...] = _residual_out(x1_sc, g2_ref, acc_sc[...], fg_ref)


def _outproj_ffn(x, mconv, gate, ya, mod, ng, w_ba, w_o, wg, wu, wd, final_g, *, tm, tf, seq_len):
    n, d = x.shape
    dff = wg.shape[1]
    tiles_per_seq = max(seq_len // tm, 1)
    rows = mod.shape[1]
    final = final_g is not None
    tile = pl.BlockSpec((tm, d), lambda i, j: (i, 0))
    mod_piece = lambda p: pl.BlockSpec((None, rows, d), lambda i, j: (i // tiles_per_seq, 0, p))
    in_specs = [tile, tile, tile, tile, mod_piece(2), mod_piece(3), mod_piece(4),
                _resident((1, d)), _resident(w_ba.shape), _resident(w_o.shape), mod_piece(5),
                pl.BlockSpec((d, tf), lambda i, j: (0, j)), pl.BlockSpec((d, tf), lambda i, j: (0, j)),
                pl.BlockSpec((tf, d), lambda i, j: (j, 0))]
    args = [x, mconv, gate, ya, mod, mod, mod, ng, w_ba, w_o, mod, wg, wu, wd]
    if final:
        in_specs.append(pl.BlockSpec((1, d), lambda i, j: (0, 0)))
        args.append(final_g)
    return pl.pallas_call(
        functools.partial(_outproj_ffn_kernel, final=final),
        grid=(n // tm, dff // tf), in_specs=in_specs, out_specs=tile,
        out_shape=jax.ShapeDtypeStruct((n, d), F32),
        scratch_shapes=[pltpu.VMEM((tm, d), F32), pltpu.VMEM((tm, d), BF16), pltpu.VMEM((tm, d), F32)],
        compiler_params=_cparams(("arbitrary", "arbitrary")),
        name="outproj_ffn",
    )(*args)


MOE_CHUNK = 128


def _moe_kernel(*refs, final, nj):
    if final:
        (x1_ref, h2_ref, g2_ref, comb_ref, tri_ref, wg_ref, wu_ref, wd_ref, fg_ref,
         o_ref, rank_sc, xc_sc, y_sc, nch_sm) = refs
    else:
        (x1_ref, h2_ref, g2_ref, comb_ref, tri_ref, wg_ref, wu_ref, wd_ref,
         o_ref, rank_sc, xc_sc, y_sc, nch_sm) = refs
        fg_ref = None
    e = pl.program_id(1)
    j = pl.program_id(2)
    ne = pl.num_programs(1)
    tb = h2_ref.shape[0]

    @pl.when(jnp.logical_and(e == 0, j == 0))
    def _():
        routed = comb_ref[...] > 0.0
        cnt = jnp.dot(jnp.where(routed, 1.0, 0.0).astype(BF16), tri_ref[...], preferred_element_type=F32)
        rank_sc[...] = jnp.where(routed, cnt, -1.0)
        o_ref[...] = jnp.zeros(o_ref.shape, F32)

    rank = rank_sc[pl.ds(e, 1), :]

    def one_hot(c):
        slot = (lax.broadcasted_iota(jnp.int32, (MOE_CHUNK, tb), 0) + c * MOE_CHUNK).astype(F32)
        return rank == slot

    @pl.when(j == 0)
    def _():
        n_tok = jnp.sum(jnp.where(rank >= 0.0, 1.0, 0.0)).astype(jnp.int32)
        nch = (n_tok + (MOE_CHUNK - 1)) // MOE_CHUNK
        nch_sm[0] = nch

        def body(c, carry):
            r0 = pl.multiple_of(c * MOE_CHUNK, MOE_CHUNK)
            sel = jnp.where(one_hot(c), 1.0, 0.0).astype(BF16)
            xc = jnp.dot(sel, h2_ref[...], preferred_element_type=F32).astype(BF16)
            xc_sc[pl.ds(r0, MOE_CHUNK), :] = xc
            y_sc[pl.ds(r0, MOE_CHUNK), :] = _swiglu(xc, wg_ref, wu_ref, wd_ref)
            return carry

        lax.fori_loop(0, nch, body, 0)

    if nj > 2:
        @pl.when(jnp.logical_and(j > 0, j < nj - 1))
        def _():
            def body(c, carry):
                r0 = pl.multiple_of(c * MOE_CHUNK, MOE_CHUNK)
                y_sc[pl.ds(r0, MOE_CHUNK), :] += _swiglu(xc_sc[pl.ds(r0, MOE_CHUNK), :], wg_ref, wu_ref, wd_ref)
                return carry

            lax.fori_loop(0, nch_sm[0], body, 0)

    @pl.when(j == nj - 1)
    def _():
        w_e = comb_ref[pl.ds(e, 1), :]

        def body(c, carry):
            r0 = pl.multiple_of(c * MOE_CHUNK, MOE_CHUNK)
            y = y_sc[pl.ds(r0, MOE_CHUNK), :] + _swiglu(xc_sc[pl.ds(r0, MOE_CHUNK), :], wg_ref, wu_ref, wd_ref)
            sel = jnp.where(one_hot(c), 1.0, 0.0)
            w_row = jnp.sum(sel * w_e, axis=1, keepdims=True)
            yw = (y * w_row).astype(BF16)
            o_ref[...] += jnp.dot(sel.T.astype(BF16), yw, preferred_element_type=F32)
            return carry

        lax.fori_loop(0, nch_sm[0], body, 0)

    @pl.when(jnp.logical_and(e == ne - 1, j == nj - 1))
    def _():
        o_ref[...] = _residual_out(x1_ref, g2_ref, o_ref[...], fg_ref)


def _moe(x1, h2, mod, comb_t, wg, wu, wd, final_g, *, tb, tf, seq_len):
    n, d = x1.shape
    ne, _, dff = wg.shape
    nj = dff // tf
    assert nj >= 2 and tb % MOE_CHUNK == 0 and n % tb == 0
    blocks_per_seq = max(seq_len // tb, 1)
    rows = mod.shape[1]
    final = final_g is not None
    pos = jnp.arange(tb)
    tri = (pos[:, None] < pos[None, :]).astype(BF16)
    tile = lambda i, e, j: (i, 0)
    in_specs = [pl.BlockSpec((tb, d), tile, pipeline_mode=pl.Buffered(1)),
                pl.BlockSpec((tb, d), tile),
                pl.BlockSpec((None, rows, d), lambda i, e, j: (i // blocks_per_seq, 0, 5)),
                pl.BlockSpec((ne, tb), lambda i, e, j: (0, i)),
                pl.BlockSpec((tb, tb), lambda i, e, j: (0, 0), pipeline_mode=pl.Buffered(1)),
                pl.BlockSpec((None, d, tf), lambda i, e, j: (e, 0, j)),
                pl.BlockSpec((None, d, tf), lambda i, e, j: (e, 0, j)),
                pl.BlockSpec((None, tf, d), lambda i, e, j: (e, j, 0))]
    args = [x1, h2, mod, comb_t, tri, wg, wu, wd]
    if final:
        in_specs.append(pl.BlockSpec((1, d), lambda i, e, j: (0, 0)))
        args.append(final_g)
    return pl.pallas_call(
        functools.partial(_moe_kernel, final=final, nj=nj),
        grid=(n // tb, ne, nj), in_specs=in_specs, out_specs=pl.BlockSpec((tb, d), tile),
        out_shape=jax.ShapeDtypeStruct((n, d), F32),
        scratch_shapes=[pltpu.VMEM((ne, tb), F32), pltpu.VMEM((tb, d), BF16), pltpu.VMEM((tb, d), F32),
                        pltpu.SMEM((1,), jnp.int32)],
        compiler_params=_cparams(("arbitrary", "arbitrary", "arbitrary")),
        name="moe",
    )(*args)


def _rope_tables(pos):
    half = ROT_DIM // 2
    inv = ROPE_THETA ** (-jnp.arange(half, dtype=F32) / half)
    ang = pos.astype(F32)[:, None] * inv[None, :]
    cos, sin = jnp.cos(ang), jnp.sin(ang)
    r = jnp.arange(LANES) % HEAD_DIM
    idx = r % half
    lo = (r < half)[None, :]
    hi = ((r >= half) & (r < ROT_DIM))[None, :]
    c = jnp.where((r < ROT_DIM)[None, :], cos[:, idx], 1.0)
    sa = jnp.where(lo, -sin[:, idx], 0.0)
    sb = jnp.where(hi, sin[:, idx], 0.0)
    return c, sa, sb


def _pick_ff_tile(dff, cap):
    best = None
    for t in range(LANES, cap + 1, LANES):
        if dff % t == 0:
            best = t
    return best if best is not None else dff


def kernel(x_prompt, x_sample, cache_k, cache_v, state_conv, page_table, c_prompt, c_sample, w_ada, b_ada, norm1_g, norm2_g, w_in, conv_w, lambda_q1, lambda_k1, lambda_q2, lambda_k2, subln_g, w_br_conv, w_br_attn, w_o, w_ff_gate, w_ff_up, w_ff_down, w_router, w_moe_gate, w_moe_up, w_moe_down, final_g):
    nseq, seq_len, d = x_prompt.shape
    nb, dec_seq, _ = x_sample.shape
    depth = w_ada.shape[0]
    page = cache_k.shape[2]
    past_len = page_table.shape[1] * page
    kvw = N_KV_HEADS * V_DIM
    n_p, n_s = nseq * seq_len, nb * dec_seq
    tm_p = min(512, seq_len)
    tq = min(1024, seq_len)
    tk = min(512, seq_len)
    tb_p = min(1024, seq_len)
    assert dec_seq == 8 and d % LANES == 0 and n_s % MOE_CHUNK == 0

    mod_all = _ada(jnp.concatenate([c_prompt, c_sample], axis=0), w_ada, b_ada)
    tabs_p = _rope_tables(jnp.arange(seq_len))
    tabs_s = tuple(jnp.tile(t, (nb, 1)) for t in _rope_tables(past_len + jnp.arange(dec_seq)))
    final_row = final_g.reshape(1, d)
    eye_kv = jnp.eye(N_KV_HEADS, dtype=BF16)

    xp = x_prompt.reshape(n_p, d)
    xs = x_sample.reshape(n_s, d)
    outs = [[] for _ in range(6)]
    new_kv_p = None
    for l in range(depth):
        lam_init = 0.8 - 0.6 * math.exp(-0.3 * l)
        mod_p = mod_all[l, :nseq].reshape(nseq, 1, 6 * d)
        mod_s = jnp.repeat(mod_all[l, nseq:], dec_seq, axis=0).reshape(1, n_s, 6 * d)
        w_in_l = w_in[l].astype(BF16)
        w_brc = w_br_conv[l].astype(BF16)
        w_bra = w_br_attn[l].astype(BF16)
        w_o_l = w_o[l].astype(BF16)
        ng1 = norm1_g[l].reshape(1, d)
        ng2 = norm2_g[l].reshape(1, d)
        lam_vecs = jnp.stack([lambda_q1[l], lambda_k1[l], lambda_q2[l], lambda_k2[l]])
        sg = subln_g[l].reshape(1, V_DIM)
        st = state_conv[l]
        state = (jnp.repeat(st[:, 0], dec_seq, axis=0), jnp.repeat(st[:, 1], dec_seq, axis=0))
        moe = l % 2 == 1
        i = l // 2
        if moe:
            wr = w_router[i].T
            wg, wu, wd = w_moe_gate[i].astype(BF16), w_moe_up[i].astype(BF16), w_moe_down[i].astype(BF16)
        else:
            wr = None
            wg, wu, wd = w_ff_gate[i].astype(BF16), w_ff_up[i].astype(BF16), w_ff_down[i].astype(BF16)
        tf = _pick_ff_tile(wg.shape[-1], 1792)
        fin = final_row if l == depth - 1 else None

        def merge_and_mix(x, mconv, gate, ya, mod, tm, tb, seq):
            if moe:
                res = _outproj(x, mconv, gate, ya, mod, ng2, w_bra, w_o_l, wr, tm=tm, seq_len=seq)
                return _moe(res[0], res[1], mod, res[2], wg, wu, wd, fin, tb=tb, tf=tf, seq_len=seq)
            return _outproj_ffn(x, mconv, gate, ya, mod, ng2, w_bra, w_o_l, wg, wu, wd, fin,
                                tm=tm, tf=tf, seq_len=seq)

        mconv, gate, q, kf, vf, kb, vt, cs = _inproj(
            xp, mod_p, ng1, w_in_l, w_brc, conv_w[l], tabs_p, None, tm=tm_p, seq_len=seq_len, sample=False, tk=tk,
            layer=l, depth=depth, carried=new_kv_p)
        new_kv_p = (kf, vf)
        ya = _attn_prompt(q, kb, vt, lam_vecs, sg.reshape(V_DIM, 1), nseq=nseq, seq_len=seq_len, tq=tq,
                          lam_init=lam_init)
        xp = merge_and_mix(xp, mconv, gate, ya, mod_p, tm_p, tb_p, seq_len)
        outs[2].append(cs)

        mconv, gate, q, kf, vf, kb, vb, u = _inproj(
            xs, mod_s, ng1, w_in_l, w_brc, conv_w[l], tabs_s, state, tm=n_s, seq_len=n_s, sample=True)
        qe = q.reshape(N_KV_HEADS, GROUP, 2, nb, dec_seq, LANES).transpose(3, 0, 2, 1, 4, 5)
        qbd = (qe[:, :, :, :, :, None, :] * eye_kv[None, :, None, None, None, :, None]).reshape(
            nb, N_KV_HEADS * 2 * GROUP * dec_seq, kvw)
        pad = ((0, 0), (0, page - dec_seq), (0, 0))
        kn = jnp.pad(kb.reshape(nb, dec_seq, kvw), pad)
        vn = jnp.pad(vb.reshape(nb, dec_seq, kvw), pad)
        o = _attn_sample(page_table, qbd, kn, vn, lam_vecs, sg, cache_k, cache_v,
                         layer=l, pages=min(32, page_table.shape[1]), dec_seq=dec_seq, lam_init=lam_init)
        ya = o.reshape(nb, N_KV_HEADS, GROUP, dec_seq, V_DIM).transpose(0, 3, 1, 2, 4).reshape(n_s, d).astype(BF16)
        xs = merge_and_mix(xs, mconv, gate, ya, mod_s, n_s, n_s, n_s)
        outs[3].append(kf.reshape(nb, dec_seq, N_KV_HEADS, V_DIM))
        outs[4].append(vf.reshape(nb, dec_seq, N_KV_HEADS, V_DIM))
        outs[5].append(u.reshape(nb, dec_seq, d)[:, dec_seq - 2:])

    kv_shape = (depth, nseq, seq_len, N_KV_HEADS, V_DIM)
    return (xp.reshape(nseq, seq_len, d), xs.reshape(nb, dec_seq, d),
            new_kv_p[0].reshape(kv_shape), new_kv_p[1].reshape(kv_shape), jnp.stack(outs[2]),
            jnp.stack(outs[3]), jnp.stack(outs[4]), jnp.stack(outs[5]))
```
